```python
import numpy as np
import jax, jax.numpy as jnp
from jax import lax

D_MODEL = 1024
BATCH = 4
SEQ = 4096
DEPTH = 2

GRID_W = 64
WIN_R = 8
WIN_C = 16
NA_HEADS = 8
NA_HEAD_DIM = D_MODEL // 16
NA_WIDTH = NA_HEADS * NA_HEAD_DIM
GLA_HEADS = 4
GLA_DK = D_MODEL // 16
GLA_DV = D_MODEL // 8
GLA_KEY_WIDTH = GLA_HEADS * GLA_DK
GLA_VAL_WIDTH = GLA_HEADS * GLA_DV
GLA_RANK = 16
GLA_TAU = 16.0
HGRN_HEADS = 4
HGRN_DK = 128
HGRN_DV = D_MODEL // 8
HGRN_KEY_WIDTH = HGRN_HEADS * HGRN_DK
HGRN_VAL_WIDTH = HGRN_HEADS * HGRN_DV
CHUNK = 16
N_BRANCHES = 3
PROJ_SIZES = (NA_WIDTH, NA_WIDTH, NA_WIDTH,
              GLA_KEY_WIDTH, GLA_KEY_WIDTH, GLA_VAL_WIDTH, GLA_VAL_WIDTH, 2 * GLA_RANK,
              HGRN_KEY_WIDTH, 2 * HGRN_KEY_WIDTH, HGRN_VAL_WIDTH, HGRN_VAL_WIDTH,
              N_BRANCHES * D_MODEL)
PROJ_WIDTH = sum(PROJ_SIZES)
FFN_HIDDEN = ((8 * D_MODEL // 3 + 255) // 256) * 256
N_MOD = 6
RMS_EPS = 1e-6
NEG_INF = -1e30
F_FLOOR = 1e-30

kernel_name = 'hybrid_na_gla_hgrn2_encoder'


def rmsnorm(x, w):
    xf = x.astype(jnp.float32)
    y = xf * lax.rsqrt(jnp.mean(xf * xf, axis=-1, keepdims=True) + RMS_EPS)
    return (y * w.astype(jnp.float32)).astype(x.dtype)


def to_heads(a, n):
    b, t, _ = a.shape
    return a.reshape(b, t, n, -1).transpose(0, 2, 1, 3)


def from_heads(a):
    b, h, t, d = a.shape
    return a.transpose(0, 2, 1, 3).reshape(b, t, h * d)


def neighborhood_attention(q, k, v, rpb):
    bsz, t, _ = q.shape
    rows = t // GRID_W
    wr = min(WIN_R, rows)
    def grid(a):
        return a.reshape(bsz, rows, GRID_W, NA_HEADS, NA_HEAD_DIM).transpose(0, 3, 1, 2, 4)
    qg = grid(q) * (NA_HEAD_DIM ** -0.5)
    kg, vg = grid(k), grid(v)
    r = jnp.arange(rows)
    row_start = jnp.clip(r - WIN_R // 2, 0, rows - wr)
    row_idx = row_start[:, None] + jnp.arange(wr)[None, :]
    k_band = kg[:, :, row_idx]
    v_band = vg[:, :, row_idx]
    col = jnp.arange(GRID_W)
    col_start = jnp.clip(col - WIN_C // 2, 0, GRID_W - WIN_C)
    col_mask = (col[None, :] >= col_start[:, None]) & (col[None, :] < col_start[:, None] + WIN_C)
    row_off = row_idx - r[:, None] + (WIN_R - 1)
    col_off = jnp.clip(col[None, :] - col[:, None], -(WIN_C - 1), WIN_C - 1) + (WIN_C - 1)
    bias = rpb[:, row_off[:, None, :, None], col_off[None, :, None, :]]
    s = jnp.einsum('bhrqd,bhrjkd->bhrqjk', qg, k_band).astype(jnp.float32) + bias.astype(jnp.float32)
    s = jnp.where(col_mask[:, None, :], s, NEG_INF)
    p = jax.nn.softmax(s.reshape(bsz, NA_HEADS, rows, GRID_W, wr * GRID_W), axis=-1)
    p = p.reshape(s.shape).astype(v.dtype)
    o = jnp.einsum('bhrqjk,bhrjkd->bhrqd', p, v_band)
    return o.transpose(0, 2, 3, 1, 4).reshape(bsz, t, NA_WIDTH)


def chunk_gated_linear_attention(q, k, v, log_a):
    bsz, nh, t, dk = q.shape
    dv = v.shape[-1]
    n = t // CHUNK
    q, k, v, log_a = [a.astype(jnp.float32).reshape(bsz, nh, n, CHUNK, a.shape[-1]) for a in (q, k, v, log_a)]
    b = jnp.cumsum(log_a, axis=3)
    tri = jnp.tril(jnp.ones((CHUNK, CHUNK), dtype=bool))[:, :, None]
    diff = b[:, :, :, :, None, :] - b[:, :, :, None, :, :]
    decay = jnp.where(tri, jnp.exp(jnp.where(tri, diff, 0.0)), 0.0)
    scores = jnp.einsum('bhnid,bhnjd,bhnijd->bhnij', q, k, decay)
    o_intra = jnp.einsum('bhnij,bhnjv->bhniv', scores, v)
    b_last = b[:, :, :, -1:, :]
    q_dec = q * jnp.exp(b)
    k_dec = k * jnp.exp(b_last - b)
    a_chunk = jnp.exp(b_last[:, :, :, 0, :])
    def step(state, xs):
        qn, kn, vn, an = xs
        o = jnp.einsum('bhid,bhdv->bhiv', qn, state)
        state = an[..., None] * state + jnp.einsum('bhjd,bhjv->bhdv', kn, vn)
        return state, o
    xs = tuple(jnp.moveaxis(a, 2, 0) for a in (q_dec, k_dec, v, a_chunk))
    _, o_inter = lax.scan(step, jnp.zeros((bsz, nh, dk, dv), jnp.float32), xs)
    o = o_intra + jnp.moveaxis(o_inter, 0, 2)
    return o.reshape(bsz, nh, t, dv)


def bidirectional_gla(q, k_fwd, k_bwd, v, la_fwd, la_bwd):
    flip = lambda a: jnp.flip(a, axis=2)
    fwd = chunk_gated_linear_attention(q, k_fwd, v, la_fwd)
    bwd = chunk_gated_linear_attention(flip(q), flip(k_bwd), flip(v), flip(la_bwd))
    return fwd + flip(bwd)


def gla_branch(q, k, v, g, lr, lr_up, lr_bias, norm_w):
    bsz, t, _ = q.shape
    z = jnp.einsum('btsr,srk->btsk', lr.reshape(bsz, t, 2, GLA_RANK), lr_up) + lr_bias
    log_a = jax.nn.log_sigmoid(z.astype(jnp.float32)) / GLA_TAU
    la_f = to_heads(log_a[:, :, 0], GLA_HEADS)
    la_b = to_heads(log_a[:, :, 1], GLA_HEADS)
    qh = to_heads(q, GLA_HEADS) * (GLA_DK ** -0.5)
    kh = to_heads(k, GLA_HEADS)
    vh = to_heads(v, GLA_HEADS)
    o = bidirectional_gla(qh, kh, kh, vh, la_f, la_b)
    o = rmsnorm(o, norm_w) * jax.nn.silu(to_heads(g, GLA_HEADS).astype(jnp.float32))
    return from_heads(o).astype(q.dtype)


def hgrn2_branch(q, f, i, g, lb, norm_w):
    bsz, t, _ = q.shape
    z = f.reshape(bsz, t, 2, HGRN_KEY_WIDTH).astype(jnp.float32)
    lbf = lb.astype(jnp.float32)
    forget = lbf + (1.0 - lbf) * jax.nn.sigmoid(z)
    log_f = jnp.log(jnp.maximum(forget, F_FLOOR))
    k_in = (1.0 - lbf) * jax.nn.sigmoid(-z)
    qh = to_heads(jax.nn.silu(q), HGRN_HEADS)
    vh = to_heads(i, HGRN_HEADS)
    o = bidirectional_gla(qh,
                          to_heads(k_in[:, :, 0], HGRN_HEADS), to_heads(k_in[:, :, 1], HGRN_HEADS), vh,
                          to_heads(log_f[:, :, 0], HGRN_HEADS), to_heads(log_f[:, :, 1], HGRN_HEADS))
    o = rmsnorm(o, norm_w) * jax.nn.silu(to_heads(g, HGRN_HEADS).astype(jnp.float32))
    return from_heads(o).astype(q.dtype)


def setup_inputs(seed: int = 0) -> dict:
    key = jax.random.key(seed)
    ks = jax.random.split(key, 24)
    nrm = lambda k, shape, scale: jax.random.normal(k, shape, jnp.float32) * scale
    return {
        'x': nrm(ks[0], (BATCH, SEQ, D_MODEL), 1.0),
        'c': nrm(ks[1], (BATCH, D_MODEL), 1.0),
        'w_ada': nrm(ks[2], (DEPTH, D_MODEL, N_MOD * D_MODEL), D_MODEL ** -0.5),
        'b_ada': nrm(ks[3], (DEPTH, N_MOD * D_MODEL), 0.02),
        'norm1_w': 1.0 + nrm(ks[4], (DEPTH, D_MODEL), 0.02),
        'w_in': nrm(ks[5], (DEPTH, D_MODEL, PROJ_WIDTH), D_MODEL ** -0.5),
        'na_rpb': nrm(ks[6], (DEPTH, NA_HEADS, 2 * WIN_R - 1, 2 * WIN_C - 1), 0.1),
        'gla_lr_up': nrm(ks[7], (DEPTH, 2, GLA_RANK, GLA_KEY_WIDTH), GLA_RANK ** -0.5),
        'gla_lr_bias': nrm(ks[8], (DEPTH, 2, GLA_KEY_WIDTH), 0.02),
        'gla_norm_w': 1.0 + nrm(ks[9], (DEPTH, GLA_DV), 0.02),
        'hgrn_lb_logits': nrm(ks[10], (DEPTH, 2 * HGRN_KEY_WIDTH), 1.0),
        'hgrn_norm_w': 1.0 + nrm(ks[11], (DEPTH, HGRN_DV), 0.02),
        'w_proj_na': nrm(ks[12], (DEPTH, NA_WIDTH, D_MODEL), NA_WIDTH ** -0.5),
        'w_proj_gla': nrm(ks[13], (DEPTH, GLA_VAL_WIDTH, D_MODEL), GLA_VAL_WIDTH ** -0.5),
        'w_proj_hgrn': nrm(ks[14], (DEPTH, HGRN_VAL_WIDTH, D_MODEL), HGRN_VAL_WIDTH ** -0.5),
        'w_out': nrm(ks[15], (DEPTH, D_MODEL, D_MODEL), D_MODEL ** -0.5),
        'norm2_w': 1.0 + nrm(ks[16], (DEPTH, D_MODEL), 0.02),
        'w_ffn_gate': nrm(ks[17], (DEPTH, D_MODEL, FFN_HIDDEN), D_MODEL ** -0.5),
        'w_ffn_up': nrm(ks[18], (DEPTH, D_MODEL, FFN_HIDDEN), D_MODEL ** -0.5),
        'w_ffn_down': nrm(ks[19], (DEPTH, FFN_HIDDEN, D_MODEL), FFN_HIDDEN ** -0.5),
        'final_norm_w': 1.0 + nrm(ks[20], (D_MODEL,), 0.02),
    }


def reference(x, c, w_ada, b_ada, norm1_w, w_in, na_rpb, gla_lr_up, gla_lr_bias, gla_norm_w,
              hgrn_lb_logits, hgrn_norm_w, w_proj_na, w_proj_gla, w_proj_hgrn, w_out, norm2_w,
              w_ffn_gate, w_ffn_up, w_ffn_down, final_norm_w):
    split_points = [int(s) for s in np.cumsum(PROJ_SIZES)[:-1]]
    lb_p = jax.nn.softmax(hgrn_lb_logits.astype(jnp.float32), axis=0)
    lb_all = jnp.clip(jnp.cumsum(lb_p, axis=0) - lb_p[0], 0.0, 1.0)
    c_act = jax.nn.silu(c)
    for l in range(DEPTH):
        mod = c_act @ w_ada[l] + b_ada[l]
        shift1, scale1, gate1, shift2, scale2, gate2 = jnp.split(mod[:, None, :], N_MOD, axis=-1)
        h = rmsnorm(x, norm1_w[l]) * (1.0 + scale1) + shift1
        p = h @ w_in[l]
        (na_q, na_k, na_v, gla_q, gla_k, gla_v, gla_g, gla_lr,
         hg_q, hg_f, hg_i, hg_g, gates) = jnp.split(p, split_points, axis=-1)
        o_na = neighborhood_attention(na_q, na_k, na_v, na_rpb[l])
        o_gla = gla_branch(gla_q, gla_k, gla_v, gla_g, gla_lr, gla_lr_up[l], gla_lr_bias[l], gla_norm_w[l])
        o_hg = hgrn2_branch(hg_q, hg_f, hg_i, hg_g, lb_all[l].reshape(2, HGRN_KEY_WIDTH), hgrn_norm_w[l])
        g_na, g_gla, g_hg = jnp.split(jax.nn.sigmoid(gates), N_BRANCHES, axis=-1)
        merged = (g_na * (o_na @ w_proj_na[l]) + g_gla * (o_gla @ w_proj_gla[l])
                  + g_hg * (o_hg @ w_proj_hgrn[l]))
        x = x + gate1 * (merged @ w_out[l])
        h = rmsnorm(x, norm2_w[l]) * (1.0 + scale2) + shift2
        ff = (jax.nn.silu(h @ w_ffn_gate[l]) * (h @ w_ffn_up[l])) @ w_ffn_down[l]
        x = x + gate2 * ff
    return rmsnorm(x, final_norm_w)
```

```python
import functools

import jax
import jax.numpy as jnp
import numpy as np
from jax import lax
from jax.experimental import pallas as pl
from jax.experimental.pallas import tpu as pltpu

F32 = jnp.float32
BF16 = jnp.bfloat16

D_MODEL = 1024
GRID_W = 64
WIN_R = 8
WIN_C = 16
NA_HEADS = 8
NA_HEAD_DIM = 64
NA_WIDTH = 512
GLA_HEADS = 4
GLA_DK = 64
GLA_RANK = 16
GLA_TAU = 16.0
HGRN_KEY_WIDTH = 512
N_MOD = 6
FFN_HIDDEN = 2816
RMS_EPS = 1e-6
NEG_INF = -1e30
F_FLOOR = 1e-30

PCHUNK = 512
(CH_NA_Q, CH_NA_K, CH_NA_V, CH_GLA_Q, CH_GLA_K, CH_GLA_V, CH_GLA_G,
 CH_HG_Q, CH_HG_FF, CH_HG_FB, CH_HG_I, CH_HG_G, CH_GATES) = range(13)
N_GATE_CHUNKS = 6
N_PCHUNKS = CH_GATES + N_GATE_CHUNKS
LR_PAD = 128
HEAD_LANES = 128
SUBLANES = 8
N_REC_HEADS = 4

VMEM_LIMIT_BYTES = 56 * 1024 * 1024

REC_TOKENS_PER_STEP = 512
DECAY_SPAN_LIMIT = 60.0


def _cparams(sem):
    return pltpu.CompilerParams(dimension_semantics=sem, vmem_limit_bytes=VMEM_LIMIT_BYTES)


def _resident(shape):
    nd = len(shape)
    return pl.BlockSpec(shape, lambda *_: (0,) * nd, pipeline_mode=pl.Buffered(1))


def _dot(a, b):
    return jnp.dot(a, b, preferred_element_type=F32)


def _dot_nt(a, b):
    return lax.dot_general(a, b, (((1,), (1,)), ((), ())), preferred_element_type=F32)


def _dot_tn(a, b):
    return lax.dot_general(a, b, (((0,), (0,)), ((), ())), preferred_element_type=F32)


def _sigmoid_pair(z):
    e = jnp.exp(-jnp.abs(z))
    r = 1.0 / (1.0 + e)
    er = e * r
    pos = z >= 0
    return jnp.where(pos, r, er), jnp.where(pos, er, r)


def _silu(z):
    return z * _sigmoid_pair(z)[0]


def _split_bf16(a):
    hi = a.astype(BF16)
    lo = (a - hi.astype(F32)).astype(BF16)
    return hi, lo


MOD_COLS = 1536


def _mod_kernel(c_ref, w_ref, b_ref, o_ref):
    c_act = _silu(c_ref[...])
    c_hi, c_lo = _split_bf16(c_act)
    w_hi, w_lo = _split_bf16(w_ref[0])
    acc = _dot(c_hi, w_hi) + _dot(c_lo, w_hi) + _dot(c_hi, w_lo)
    o_ref[0] = acc + b_ref[0]


def _modulation(c, w_ada, b_ada):
    depth, d, width = w_ada.shape
    bsz = c.shape[0]
    rows = 8
    c_pad = jnp.zeros((rows, d), F32).at[:bsz].set(c)
    out = pl.pallas_call(
        _mod_kernel,
        grid=(depth, width // MOD_COLS),
        in_specs=[
            pl.BlockSpec((rows, d), lambda l, j: (0, 0)),
            pl.BlockSpec((1, d, MOD_COLS), lambda l, j: (l, 0, j)),
            pl.BlockSpec((1, 1, MOD_COLS), lambda l, j: (l, 0, j)),
        ],
        out_specs=pl.BlockSpec((1, rows, MOD_COLS), lambda l, j: (l, 0, j)),
        out_shape=jax.ShapeDtypeStruct((depth, rows, width), F32),
        compiler_params=_cparams(("arbitrary", "arbitrary")),
        name="adaln_mod",
    )(c_pad, w_ada, b_ada.reshape(depth, 1, width))
    return out[:, :bsz]


INPROJ_ROWS = 512


def _rmsnorm_mod(x, w, scale, shift):
    ms = jnp.mean(x * x, axis=-1, keepdims=True)
    y = x * lax.rsqrt(ms + RMS_EPS) * w
    return y * (1.0 + scale) + shift


def _inproj_kernel(x_ref, mod_ref, nw_ref, w_ref, wlr_ref, p_ref, lr_ref, h_ref):
    d = D_MODEL
    shift = mod_ref[0, :, 0:d]
    scale = mod_ref[0, :, d:2 * d]
    h_ref[...] = _rmsnorm_mod(x_ref[...], nw_ref[...], scale, shift).astype(BF16)

    def body(j, carry):
        p_ref[j] = _dot(h_ref[...], w_ref[j]).astype(BF16)
        return carry

    lax.fori_loop(0, N_PCHUNKS, body, 0)
    lr_ref[...] = _dot(h_ref[...], wlr_ref[...]).astype(BF16)


def _in_projection(x2, mod_l, norm_w, w_chunks, w_lr, tokens_per_batch):
    n, d = x2.shape
    tm = INPROJ_ROWS
    per_batch = tokens_per_batch // tm
    return pl.pallas_call(
        _inproj_kernel,
        grid=(n // tm,),
        in_specs=[
            pl.BlockSpec((tm, d), lambda i: (i, 0)),
            pl.BlockSpec((1, 1, N_MOD * d), lambda i: (i // per_batch, 0, 0)),
            _resident((1, d)),
            _resident((N_PCHUNKS, d, PCHUNK)),
            _resident((d, LR_PAD)),
        ],
        out_specs=[
            pl.BlockSpec((N_PCHUNKS, tm, PCHUNK), lambda i: (0, i, 0)),
            pl.BlockSpec((tm, LR_PAD), lambda i: (i, 0)),
        ],
        out_shape=[
            jax.ShapeDtypeStruct((N_PCHUNKS, n, PCHUNK), BF16),
            jax.ShapeDtypeStruct((n, LR_PAD), BF16),
        ],
        scratch_shapes=[pltpu.VMEM((tm, d), BF16)],
        compiler_params=_cparams(("arbitrary",)),
        name="in_proj",
    )(x2, mod_l, norm_w, w_chunks, w_lr)


NA_ROWS_PER_STEP = 8
NA_BAND = WIN_R * GRID_W
NA_VARIANTS = 8


def _na_kernel(q_ref, k_ref, v_ref, bias_ref, o_ref, *, rows_total):
    step = pl.program_id(1)
    lane = lax.broadcasted_iota(jnp.int32, (GRID_W, HEAD_LANES), 1)
    low = lane < NA_HEAD_DIM

    def row_body(rr, carry):
        r = step * NA_ROWS_PER_STEP + rr
        row_start = jnp.clip(r - WIN_R // 2, 0, rows_total - WIN_R)
        variant = jnp.where(r < WIN_R // 2, r,
                            jnp.where(r > rows_total - WIN_R // 2, r - (rows_total - WIN_R), WIN_R // 2))
        k0 = pl.multiple_of(row_start * GRID_W, GRID_W)
        q0 = pl.multiple_of(rr * GRID_W, GRID_W)
        for hp in range(NA_HEADS // 2):
            hs = slice(hp * HEAD_LANES, (hp + 1) * HEAD_LANES)
            qp = q_ref[0, pl.ds(q0, GRID_W), hs]
            kp = k_ref[0, pl.ds(k0, NA_BAND), hs]
            vp = v_ref[0, pl.ds(k0, NA_BAND), hs]
            outs = []
            for sub in range(2):
                qm = jnp.where(low if sub == 0 else jnp.logical_not(low), qp, jnp.zeros_like(qp))
                s = _dot_nt(qm, kp) + bias_ref[variant, 2 * hp + sub]
                m = jnp.max(s, axis=-1, keepdims=True)
                e = jnp.exp(s - m)
                denom = jnp.sum(e, axis=-1, keepdims=True)
                outs.append(_dot(e.astype(BF16), vp) / denom)
            o_ref[pl.ds(q0, GRID_W), hs] = jnp.where(low, outs[0], outs[1]).astype(BF16)
        return carry

    lax.fori_loop(0, NA_ROWS_PER_STEP, row_body, 0)


def _na_bias_table(rpb, rows):
    wr = min(WIN_R, rows)
    col = np.arange(GRID_W)
    col_start = np.clip(col - WIN_C // 2, 0, GRID_W - WIN_C)
    col_mask = (col[None, :] >= col_start[:, None]) & (col[None, :] < col_start[:, None] + WIN_C)
    col_off = np.clip(col[None, :] - col[:, None], -(WIN_C - 1), WIN_C - 1) + (WIN_C - 1)
    rep_rows = list(range(WIN_R // 2)) + [WIN_R // 2] + [rows - WIN_R + v for v in range(WIN_R // 2 + 1, WIN_R)]
    tabs = []
    for r in rep_rows:
        row_start = int(np.clip(r - WIN_R // 2, 0, rows - wr))
        row_off = row_start + np.arange(wr) - r + (WIN_R - 1)
        b = rpb[:, row_off[None, :, None], col_off[:, None, :]]
        b = jnp.where(col_mask[None, :, None, :], b.astype(F32), NEG_INF)
        tabs.append(b.reshape(NA_HEADS, GRID_W, wr * GRID_W))
    return jnp.stack(tabs)


def _neighborhood_attention(p3, bias_tab, bsz, t):
    rows = t // GRID_W
    assert rows >= 2 * WIN_R and rows % NA_ROWS_PER_STEP == 0
    n = bsz * t
    tq = NA_ROWS_PER_STEP * GRID_W
    steps = t // tq
    return pl.pallas_call(
        functools.partial(_na_kernel, rows_total=rows),
        grid=(bsz, steps),
        in_specs=[
            pl.BlockSpec((1, tq, PCHUNK), lambda b, i: (CH_NA_Q, b * steps + i, 0)),
            pl.BlockSpec((1, t, PCHUNK), lambda b, i: (CH_NA_K, b, 0)),
            pl.BlockSpec((1, t, PCHUNK), lambda b, i: (CH_NA_V, b, 0)),
            _resident((NA_VARIANTS, NA_HEADS, GRID_W, NA_BAND)),
        ],
        out_specs=pl.BlockSpec((tq, NA_WIDTH), lambda b, i: (b * steps + i, 0)),
        out_shape=jax.ShapeDtypeStruct((n, NA_WIDTH), BF16),
        compiler_params=_cparams(("arbitrary", "arbitrary")),
        name="neighborhood_attn",
    )(p3, p3, p3, bias_tab)


def _rec_kernel(*refs, mode, reverse, final, chunk, block):
    it = iter(refs)
    q_ref = next(it)
    if mode == "gla":
        k_ref, v_ref, lr_ref, uph_ref, upl_ref, gbias_ref = (next(it) for _ in range(5 + 1))
    else:
        f_ref, v_ref, lb_ref = (next(it) for _ in range(3))
    tri_ref = next(it)
    if final:
        prev_ref, gate_ref, nw_ref = (next(it) for _ in range(3))
    o_ref = next(it)
    st_ref, kbuf, wbuf, qbuf, sd_ref = (next(it) for _ in range(5))

    c, s = chunk, block
    nb = c // s
    width = N_REC_HEADS * HEAD_LANES
    n_chunks = REC_TOKENS_PER_STEP // c

    @pl.when(pl.program_id(1) == 0)
    def _():
        st_ref[...] = jnp.zeros_like(st_ref)

    ii = lax.broadcasted_iota(jnp.int32, (c, c), 0)
    jj = lax.broadcasted_iota(jnp.int32, (c, c), 1)
    bi, bj = ii // s, jj // s
    gap = (bj - bi) if reverse else (bi - bj)
    seen = (jj >= ii) if reverse else (jj <= ii)
    cls = jnp.where(gap == 0, jnp.where(seen, 0, -1), jnp.where(gap > 0, gap, -1))
    row_id = lax.broadcasted_iota(jnp.int32, (c, 1), 0)
    col_id = lax.broadcasted_iota(jnp.int32, (1, c), 1)

    def prev_block(i, steps=1):
        return i + steps if reverse else i - steps

    def chunk_body(ci, carry):
        cc = (n_chunks - 1 - ci) if reverse else ci
        r0 = pl.multiple_of(cc * c, c)
        rows = pl.ds(r0, c)

        if mode == "gla":
            q = q_ref[0, rows, :].astype(F32)
            k = k_ref[0, rows, :].astype(F32)
            lr = lr_ref[rows, :]
            z = _dot(lr, uph_ref[...]) + _dot(lr, upl_ref[...]) + gbias_ref[...]
            g = (jnp.minimum(z, 0.0) - jnp.log(1.0 + jnp.exp(-jnp.abs(z)))) * (1.0 / GLA_TAU)
        else:
            z = f_ref[0, rows, :].astype(F32)
            sig, sig_neg = _sigmoid_pair(z)
            lb = lb_ref[...]
            g = jnp.log(jnp.maximum(lb + (1.0 - lb) * sig, F_FLOOR))
            k = (1.0 - lb) * sig_neg
            q = _silu(q_ref[0, rows, :].astype(F32))
        v = v_ref[0, rows, :]

        g_hi, g_lo = _split_bf16(g)
        w = _dot(tri_ref[...], g_hi) + _dot(tri_ref[...], g_lo)
        w3 = w.reshape(nb, s, width)
        last = 0 if reverse else s - 1
        wb = [w3[i, last:last + 1, :] for i in range(nb)]
        order = list(range(nb - 1, -1, -1)) if reverse else list(range(nb))
        beta = [None] * nb
        run = jnp.zeros((1, width), F32)
        for i in order:
            beta[i] = run
            run = run + wb[i]
        total = run

        def per_block(vals):
            return jnp.concatenate([jnp.broadcast_to(x, (s, width)) for x in vals], axis=0)

        span_ok = jnp.min(w) >= -DECAY_SPAN_LIMIT
        qd = q * jnp.exp(w)
        ks = k * jnp.exp(per_block(wb) - w)
        qe = qd * jnp.exp(per_block(beta))
        ke = ks * jnp.exp(per_block([total - beta[i] - wb[i] for i in range(nb)]))
        ewb = [jnp.exp(x) for x in wb]
        q_far = []
        fac = [jnp.ones((1, width), F32)] * nb
        for dgap in range(2, nb):
            fac = [fac[i] * (ewb[prev_block(i, dgap - 1)] if 0 <= prev_block(i, dgap - 1) < nb
                             else jnp.ones((1, width), F32)) for i in range(nb)]
            q_far.append(qd * per_block(fac))

        @pl.when(span_ok)
        def _():
            kd = (k * jnp.exp(-w)).astype(BF16)
            qd16 = qd.astype(BF16)
            for h in range(N_REC_HEADS):
                hs = slice(h * HEAD_LANES, (h + 1) * HEAD_LANES)
                sd_ref[h] = _dot_nt(qd16[:, hs], kd[:, hs])

        @pl.when(jnp.logical_not(span_ok))
        def _():
            kbuf[...] = k
            wbuf[...] = w
            qbuf[...] = q
            for h in range(N_REC_HEADS):
                hs = slice(h * HEAD_LANES, (h + 1) * HEAD_LANES)

                def col_body(jg, acc):
                    j0 = pl.multiple_of(jg * SUBLANES, SUBLANES)
                    k8 = kbuf[pl.ds(j0, SUBLANES), hs]
                    w8 = wbuf[pl.ds(j0, SUBLANES), hs]
                    for jr in range(SUBLANES):
                        j = j0 + jr
                        same = (row_id // s) == (j // s)
                        valid = same & ((row_id <= j) if reverse else (row_id >= j))
                        e = jnp.exp(jnp.where(valid, wbuf[:, hs] - w8[jr:jr + 1], 0.0))
                        t = jnp.sum(qbuf[:, hs] * k8[jr:jr + 1] * e, axis=-1, keepdims=True)
                        acc = jnp.where(valid & (col_id == j), t, acc)
                    return acc

                sd_ref[h] = lax.fori_loop(0, c // SUBLANES, col_body, jnp.zeros((c, c), F32))

        ks16 = ks.astype(BF16)
        qd16 = qd.astype(BF16)
        qe16 = qe.astype(BF16)
        ke16 = ke.astype(BF16)
        qf16 = [x.astype(BF16) for x in q_far]
        decay = jnp.exp(total)
        outs = []
        for h in range(N_REC_HEADS):
            hs = slice(h * HEAD_LANES, (h + 1) * HEAD_LANES)
            a = jnp.where(cls == 0, sd_ref[h], 0.0)
            if nb > 1:
                lhs = jnp.concatenate([qd16[:, hs]] + [x[:, hs] for x in qf16], axis=0)
                so = _dot_nt(lhs, ks16[:, hs])
                for dgap in range(1, nb):
                    a = jnp.where(cls == dgap, so[(dgap - 1) * c:dgap * c], a)
            vh = v[:, hs]
            st = st_ref[h]
            o_h = _dot(a.astype(BF16), vh) + _dot_nt(qe16[:, hs], st.astype(BF16))
            st_ref[h] = st * decay[:, hs] + _dot_tn(vh, ke16[:, hs])
            outs.append(o_h)
        o = jnp.concatenate(outs, axis=1)

        if final:
            tot = prev_ref[rows, :] + o
            gate = _silu(gate_ref[0, rows, :].astype(F32))
            normed = []
            for h in range(N_REC_HEADS):
                hs = slice(h * HEAD_LANES, (h + 1) * HEAD_LANES)
                th = tot[:, hs]
                ms = jnp.mean(th * th, axis=-1, keepdims=True)
                normed.append(th * lax.rsqrt(ms + RMS_EPS) * nw_ref[...])
            o_ref[rows, :] = (jnp.concatenate(normed, axis=1) * gate).astype(o_ref.dtype)
        else:
            o_ref[rows, :] = o
        return carry

    lax.fori_loop(0, n_chunks, chunk_body, 0)


def _scan_mask(chunk, block, reverse):
    i = np.arange(chunk)
    same = (i[:, None] // block) == (i[None, :] // block)
    seen = (i[None, :] >= i[:, None]) if reverse else (i[None, :] <= i[:, None])
    return jnp.asarray((same & seen).astype(np.float32), BF16)


def _recurrent_pass(mode, reverse, final, chunk, block, p3, bsz, t, extra, prev=None, gate_chunk=None, norm_w=None):
    n = bsz * t
    tb = REC_TOKENS_PER_STEP
    nblk = t // tb
    width = N_REC_HEADS * HEAD_LANES

    def tok(b, i):
        return b * nblk + ((nblk - 1 - i) if reverse else i)

    def chunk_spec(ch):
        return pl.BlockSpec((1, tb, PCHUNK), lambda b, i: (ch, tok(b, i), 0))

    tok_spec = lambda w: pl.BlockSpec((tb, w), lambda b, i: (tok(b, i), 0))
    operands, specs = [], []
    if mode == "gla":
        lr, up_hi, up_lo, gbias = extra
        operands += [p3, p3, p3, lr, up_hi, up_lo, gbias]
        specs += [chunk_spec(CH_GLA_Q), chunk_spec(CH_GLA_K), chunk_spec(CH_GLA_V), tok_spec(LR_PAD),
                  _resident(up_hi.shape), _resident(up_lo.shape), _resident(gbias.shape)]
        gate_ch = CH_GLA_G
    else:
        (lb,) = extra
        operands += [p3, p3, p3, lb]
        specs += [chunk_spec(CH_HG_Q), chunk_spec(CH_HG_FB if reverse else CH_HG_FF), chunk_spec(CH_HG_I),
                  _resident(lb.shape)]
        gate_ch = CH_HG_G
    tri = _scan_mask(chunk, block, reverse)
    operands.append(tri)
    specs.append(_resident(tri.shape))
    if final:
        operands += [prev, p3, norm_w]
        specs += [tok_spec(width), chunk_spec(gate_ch), _resident(norm_w.shape)]
    out_dtype = BF16 if final else F32
    return pl.pallas_call(
        functools.partial(_rec_kernel, mode=mode, reverse=reverse, final=final, chunk=chunk, block=block),
        grid=(bsz, nblk),
        in_specs=specs,
        out_specs=tok_spec(width),
        out_shape=jax.ShapeDtypeStruct((n, width), out_dtype),
        scratch_shapes=[
            pltpu.VMEM((N_REC_HEADS, HEAD_LANES, HEAD_LANES), F32),
            pltpu.VMEM((chunk, width), F32),
            pltpu.VMEM((chunk, width), F32),
            pltpu.VMEM((chunk, width), F32),
            pltpu.VMEM((N_REC_HEADS, chunk, chunk), F32),
        ],
        compiler_params=_cparams(("arbitrary", "arbitrary")),
        name=f"{mode}_{'bwd' if reverse else 'fwd'}",
    )(*operands)


GLA_CHUNK, GLA_BLOCK = 128, 128
HGRN_CHUNK, HGRN_BLOCK = 64, 16


def _gla_branch(p3, lr, up_pads, gbias_pads, norm_w, bsz, t):
    fwd = _recurrent_pass("gla", False, False, GLA_CHUNK, GLA_BLOCK, p3, bsz, t,
                          (lr, up_pads[0][0], up_pads[0][1], gbias_pads[0]))
    return _recurrent_pass("gla", True, True, GLA_CHUNK, GLA_BLOCK, p3, bsz, t,
                           (lr, up_pads[1][0], up_pads[1][1], gbias_pads[1]), prev=fwd, norm_w=norm_w)


def _hgrn_branch(p3, lbs, norm_w, bsz, t):
    fwd = _recurrent_pass("hgrn", False, False, HGRN_CHUNK, HGRN_BLOCK, p3, bsz, t, (lbs[0],))
    return _recurrent_pass("hgrn", True, True, HGRN_CHUNK, HGRN_BLOCK, p3, bsz, t, (lbs[1],),
                           prev=fwd, norm_w=norm_w)


MLP_ROWS = 256


def _mlp_kernel(x_ref, ona_ref, ogla_ref, ohg_ref, gates_ref, mod_ref, wna_ref, wgla_ref, whg_ref, wout_ref,
                n2_ref, wg_ref, wu_ref, wd_ref, fn_ref, o_ref, *, last):
    d = D_MODEL

    def gate(i):
        return jnp.concatenate([gates_ref[2 * i].astype(F32), gates_ref[2 * i + 1].astype(F32)], axis=1)

    merged = (_sigmoid_pair(gate(0))[0] * _dot(ona_ref[...], wna_ref[...])
              + _sigmoid_pair(gate(1))[0] * _dot(ogla_ref[...], wgla_ref[...])
              + _sigmoid_pair(gate(2))[0] * _dot(ohg_ref[...], whg_ref[...]))
    gate1 = mod_ref[0, :, 2 * d:3 * d]
    shift2 = mod_ref[0, :, 3 * d:4 * d]
    scale2 = mod_ref[0, :, 4 * d:5 * d]
    gate2 = mod_ref[0, :, 5 * d:6 * d]
    x1 = x_ref[...] + gate1 * _dot(merged.astype(BF16), wout_ref[...])
    h = _rmsnorm_mod(x1, n2_ref[...], scale2, shift2).astype(BF16)
    act = (_silu(_dot(h, wg_ref[...])) * _dot(h, wu_ref[...])).astype(BF16)
    x2 = x1 + gate2 * _dot(act, wd_ref[...])
    if last:
        ms = jnp.mean(x2 * x2, axis=-1, keepdims=True)
        x2 = x2 * lax.rsqrt(ms + RMS_EPS) * fn_ref[...]
    o_ref[...] = x2


def _merge_mlp(x2, o_na, o_gla, o_hg, p3, mod_l, wts, tokens_per_batch, last):
    n, d = x2.shape
    tm = MLP_ROWS
    per_batch = tokens_per_batch // tm
    row = lambda w: pl.BlockSpec((tm, w), lambda i: (i, 0))
    return pl.pallas_call(
        functools.partial(_mlp_kernel, last=last),
        grid=(n // tm,),
        in_specs=[
            row(d), row(NA_WIDTH), row(NA_WIDTH), row(NA_WIDTH),
            pl.BlockSpec((N_GATE_CHUNKS, tm, PCHUNK), lambda i: (CH_GATES // N_GATE_CHUNKS, i, 0)),
            pl.BlockSpec((1, 1, N_MOD * d), lambda i: (i // per_batch, 0, 0)),
        ] + [_resident(w.shape) for w in wts],
        out_specs=row(d),
        out_shape=jax.ShapeDtypeStruct((n, d), F32),
        compiler_params=_cparams(("arbitrary",)),
        name="merge_mlp",
    )(x2, o_na, o_gla, o_hg, p3, mod_l, *wts)


def _pad_heads(w, heads, dk):
    lead = w.shape[:-1]
    w = w.reshape(*lead, heads, dk)
    w = jnp.pad(w, [(0, 0)] * len(lead) + [(0, 0), (0, HEAD_LANES - dk)])
    return w.reshape(*lead, heads * HEAD_LANES)


def _layout_w_in(w):
    sizes = (NA_WIDTH, NA_WIDTH, NA_WIDTH, 256, 256, 512, 512, 2 * GLA_RANK, 512, 1024, 512, 512, 3 * D_MODEL)
    (na_q, na_k, na_v, gq, gk, gv, gg, glr, hq, hf, hi, hg, gates) = jnp.split(w, list(np.cumsum(sizes)[:-1]), axis=1)
    na_q = na_q * (NA_HEAD_DIM ** -0.5)
    gq = _pad_heads(gq * (GLA_DK ** -0.5), GLA_HEADS, GLA_DK)
    gk = _pad_heads(gk, GLA_HEADS, GLA_DK)
    cols = jnp.concatenate([na_q, na_k, na_v, gq, gk, gv, gg, hq, hf, hi, hg, gates], axis=1)
    d = w.shape[0]
    chunks = cols.reshape(d, N_PCHUNKS, PCHUNK).transpose(1, 0, 2).astype(BF16)
    w_lr = jnp.pad(glr, ((0, 0), (0, LR_PAD - 2 * GLA_RANK))).astype(BF16)
    return chunks, w_lr


def kernel(x, c, w_ada, b_ada, norm1_w, w_in, na_rpb, gla_lr_up, gla_lr_bias, gla_norm_w, hgrn_lb_logits, hgrn_norm_w, w_proj_na, w_proj_gla, w_proj_hgrn, w_out, norm2_w, w_ffn_gate, w_ffn_up, w_ffn_down, final_norm_w):
    bsz, t, d = x.shape
    depth = w_in.shape[0]
    assert d == D_MODEL and t % REC_TOKENS_PER_STEP == 0 and t % INPROJ_ROWS == 0
    n = bsz * t

    mods = _modulation(c, w_ada, b_ada).reshape(depth, bsz, 1, N_MOD * d)
    lb_p = jax.nn.softmax(hgrn_lb_logits.astype(F32), axis=0)
    lb_all = jnp.clip(jnp.cumsum(lb_p, axis=0) - lb_p[0], 0.0, 1.0)

    xf = x.reshape(n, d)
    for l in range(depth):
        w_chunks, w_lr = _layout_w_in(w_in[l])
        p3, lr = _in_projection(xf, mods[l], norm1_w[l].reshape(1, d), w_chunks, w_lr, t)

        o_na = _neighborhood_attention(p3, _na_bias_table(na_rpb[l], t // GRID_W), bsz, t)

        up_pads, gbias_pads = [], []
        for s in range(2):
            up = jnp.zeros((LR_PAD, N_REC_HEADS * HEAD_LANES), F32)
            up = up.at[s * GLA_RANK:(s + 1) * GLA_RANK].set(_pad_heads(gla_lr_up[l, s], GLA_HEADS, GLA_DK))
            hi = up.astype(BF16)
            up_pads.append((hi, (up - hi.astype(F32)).astype(BF16)))
            gbias_pads.append(_pad_heads(gla_lr_bias[l, s].reshape(1, -1), GLA_HEADS, GLA_DK))
        o_gla = _gla_branch(p3, lr, up_pads, gbias_pads, gla_norm_w[l].reshape(1, -1), bsz, t)

        lbs = [lb_all[l, s * HGRN_KEY_WIDTH:(s + 1) * HGRN_KEY_WIDTH].reshape(1, -1) for s in range(2)]
        o_hg = _hgrn_branch(p3, lbs, hgrn_norm_w[l].reshape(1, -1), bsz, t)

        wts = [w_proj_na[l].astype(BF16), w_proj_gla[l].astype(BF16), w_proj_hgrn[l].astype(BF16),
               w_out[l].astype(BF16), norm2_w[l].reshape(1, d), w_ffn_gate[l].astype(BF16),
               w_ffn_up[l].astype(BF16), w_ffn_down[l].astype(BF16), final_norm_w.reshape(1, d)]
        xf = _merge_mlp(xf, o_na, o_gla, o_hg, p3, mods[l], wts, t, last=(l == depth - 1))
    return xf.reshape(bsz, t, d)
```

```python
import functools

import jax
import jax.numpy as jnp
import numpy as np
from jax import lax
from jax.experimental import pallas as pl
from jax.experimental.pallas import tpu as pltpu

F32 = jnp.float32
BF16 = jnp.bfloat16

D_MODEL = 1024
GRID_W = 64
WIN_R = 8
WIN_C = 16
NA_HEADS = 8
NA_HEAD_DIM = 64
NA_WIDTH = 512
GLA_HEADS = 4
GLA_DK = 64
GLA_RANK = 16
GLA_TAU = 16.0
HGRN_KEY_WIDTH = 512
N_MOD = 6
FFN_HIDDEN = 2816
RMS_EPS = 1e-6
NEG_INF = -1e30
F_FLOOR = 1e-30

PCHUNK = 512
(CH_NA_Q, CH_NA_K, CH_NA_V, CH_GLA_Q, CH_GLA_K, CH_GLA_V, CH_GLA_G,
 CH_HG_Q, CH_HG_FF, CH_HG_FB, CH_HG_I, CH_HG_G, CH_GATES) = range(13)
N_GATE_CHUNKS = 6
N_PCHUNKS = CH_GATES + N_GATE_CHUNKS
LR_PAD = 128
HEAD_LANES = 128
SUBLANES = 8
N_REC_HEADS = 4

VMEM_LIMIT_BYTES = 56 * 1024 * 1024

REC_TOKENS_PER_STEP = 512
REC_UNROLL = 4
DECAY_SPAN_LIMIT = 60.0


def _cparams(sem):
    return pltpu.CompilerParams(dimension_semantics=sem, vmem_limit_bytes=VMEM_LIMIT_BYTES)


def _resident(shape):
    nd = len(shape)
    return pl.BlockSpec(shape, lambda *_: (0,) * nd, pipeline_mode=pl.Buffered(1))


def _dot(a, b):
    return jnp.dot(a, b, preferred_element_type=F32)


def _dot_nt(a, b):
    return lax.dot_general(a, b, (((1,), (1,)), ((), ())), preferred_element_type=F32)


def _dot_tn(a, b):
    return lax.dot_general(a, b, (((0,), (0,)), ((), ())), preferred_element_type=F32)


def _sigmoid_pair(z):
    e = jnp.exp(-jnp.abs(z))
    r = 1.0 / (1.0 + e)
    er = e * r
    pos = z >= 0
    return jnp.where(pos, r, er), jnp.where(pos, er, r)


def _silu(z):
    return z * _sigmoid_pair(z)[0]


def _split_bf16(a):
    hi = a.astype(BF16)
    lo = (a - hi.astype(F32)).astype(BF16)
    return hi, lo


MOD_COLS = 1536


def _mod_kernel(c_ref, w_ref, b_ref, o_ref):
    c_act = _silu(c_ref[...])
    c_hi, c_lo = _split_bf16(c_act)
    w_hi, w_lo = _split_bf16(w_ref[0])
    acc = _dot(c_hi, w_hi) + _dot(c_lo, w_hi) + _dot(c_hi, w_lo)
    o_ref[0] = acc + b_ref[0]


def _modulation(c, w_ada, b_ada):
    depth, d, width = w_ada.shape
    bsz = c.shape[0]
    rows = 8
    c_pad = jnp.zeros((rows, d), F32).at[:bsz].set(c)
    out = pl.pallas_call(
        _mod_kernel,
        grid=(depth, width // MOD_COLS),
        in_specs=[
            pl.BlockSpec((rows, d), lambda l, j: (0, 0)),
            pl.BlockSpec((1, d, MOD_COLS), lambda l, j: (l, 0, j)),
            pl.BlockSpec((1, 1, MOD_COLS), lambda l, j: (l, 0, j)),
        ],
        out_specs=pl.BlockSpec((1, rows, MOD_COLS), lambda l, j: (l, 0, j)),
        out_shape=jax.ShapeDtypeStruct((depth, rows, width), F32),
        compiler_params=_cparams(("arbitrary", "arbitrary")),
        name="adaln_mod",
    )(c_pad, w_ada, b_ada.reshape(depth, 1, width))
    return out[:, :bsz]


INPROJ_ROWS = 512


def _rmsnorm_mod(x, w, scale, shift):
    ms = jnp.mean(x * x, axis=-1, keepdims=True)
    y = x * lax.rsqrt(ms + RMS_EPS) * w
    return y * (1.0 + scale) + shift


def _inproj_kernel(x_ref, mod_ref, nw_ref, w_ref, wlr_ref, p_ref, lr_ref, h_ref):
    d = D_MODEL
    shift = mod_ref[0, :, 0:d]
    scale = mod_ref[0, :, d:2 * d]
    h_ref[...] = _rmsnorm_mod(x_ref[...], nw_ref[...], scale, shift).astype(BF16)

    def body(j, carry):
        p_ref[j] = _dot(h_ref[...], w_ref[j]).astype(BF16)
        return carry

    lax.fori_loop(0, N_PCHUNKS, body, 0)
    lr_ref[...] = _dot(h_ref[...], wlr_ref[...]).astype(BF16)


def _in_projection(x2, mod_l, norm_w, w_chunks, w_lr, tokens_per_batch):
    n, d = x2.shape
    tm = INPROJ_ROWS
    per_batch = tokens_per_batch // tm
    return pl.pallas_call(
        _inproj_kernel,
        grid=(n // tm,),
        in_specs=[
            pl.BlockSpec((tm, d), lambda i: (i, 0)),
            pl.BlockSpec((1, 1, N_MOD * d), lambda i: (i // per_batch, 0, 0)),
            _resident((1, d)),
            _resident((N_PCHUNKS, d, PCHUNK)),
            _resident((d, LR_PAD)),
        ],
        out_specs=[
            pl.BlockSpec((N_PCHUNKS, tm, PCHUNK), lambda i: (0, i, 0)),
            pl.BlockSpec((tm, LR_PAD), lambda i: (i, 0)),
        ],
        out_shape=[
            jax.ShapeDtypeStruct((N_PCHUNKS, n, PCHUNK), BF16),
            jax.ShapeDtypeStruct((n, LR_PAD), BF16),
        ],
        scratch_shapes=[pltpu.VMEM((tm, d), BF16)],
        compiler_params=_cparams(("arbitrary",)),
        name="in_proj",
    )(x2, mod_l, norm_w, w_chunks, w_lr)


NA_ROWS_PER_STEP = 8
NA_BAND = WIN_R * GRID_W
NA_VARIANTS = 8


def _na_kernel(q_ref, k_ref, v_ref, bias_ref, o_ref, *, rows_total):
    step = pl.program_id(1)
    lane = lax.broadcasted_iota(jnp.int32, (GRID_W, HEAD_LANES), 1)
    low = lane < NA_HEAD_DIM

    def row_body(rr, carry):
        r = step * NA_ROWS_PER_STEP + rr
        row_start = jnp.clip(r - WIN_R // 2, 0, rows_total - WIN_R)
        variant = jnp.where(r < WIN_R // 2, r,
                            jnp.where(r > rows_total - WIN_R // 2, r - (rows_total - WIN_R), WIN_R // 2))
        k0 = pl.multiple_of(row_start * GRID_W, GRID_W)
        q0 = pl.multiple_of(rr * GRID_W, GRID_W)
        for hp in range(NA_HEADS // 2):
            hs = slice(hp * HEAD_LANES, (hp + 1) * HEAD_LANES)
            qp = q_ref[0, pl.ds(q0, GRID_W), hs]
            kp = k_ref[0, pl.ds(k0, NA_BAND), hs]
            vp = v_ref[0, pl.ds(k0, NA_BAND), hs]
            zero = jnp.zeros_like(qp)
            q2 = jnp.concatenate([jnp.where(low, qp, zero), jnp.where(low, zero, qp)], axis=0)
            s = _dot_nt(q2, kp) + bias_ref[variant, hp]
            m = jnp.max(s, axis=-1, keepdims=True)
            e = jnp.exp(s - m)
            denom = jnp.sum(e, axis=-1, keepdims=True)
            o2 = _dot(e.astype(BF16), vp) / denom
            o_ref[pl.ds(q0, GRID_W), hs] = jnp.where(low, o2[:GRID_W], o2[GRID_W:]).astype(BF16)
        return carry

    lax.fori_loop(0, NA_ROWS_PER_STEP, row_body, 0)


def _na_bias_table(rpb, rows):
    wr = min(WIN_R, rows)
    col = np.arange(GRID_W)
    col_start = np.clip(col - WIN_C // 2, 0, GRID_W - WIN_C)
    col_mask = (col[None, :] >= col_start[:, None]) & (col[None, :] < col_start[:, None] + WIN_C)
    col_off = np.clip(col[None, :] - col[:, None], -(WIN_C - 1), WIN_C - 1) + (WIN_C - 1)
    n_ro, n_co = 2 * WIN_R - 1, 2 * WIN_C - 1
    onehot = (col_off.reshape(-1)[None, :] == np.arange(n_co)[:, None]).astype(np.float32)
    by_col = jnp.dot(rpb.astype(F32).reshape(NA_HEADS * n_ro, n_co), onehot, precision=lax.Precision.HIGHEST)
    by_col = by_col.reshape(NA_HEADS, n_ro, GRID_W, GRID_W)
    by_col = jnp.where(col_mask[None, None], by_col, NEG_INF)
    rep_rows = list(range(WIN_R // 2)) + [WIN_R // 2] + [rows - WIN_R + v for v in range(WIN_R // 2 + 1, WIN_R)]
    tabs = []
    for r in rep_rows:
        row_start = int(np.clip(r - WIN_R // 2, 0, rows - wr))
        ro0 = row_start - r + (WIN_R - 1)
        b = by_col[:, ro0:ro0 + wr].transpose(0, 2, 1, 3)
        tabs.append(b.reshape(NA_HEADS // 2, 2 * GRID_W, wr * GRID_W))
    return jnp.stack(tabs)


def _neighborhood_attention(p3, bias_tab, bsz, t):
    rows = t // GRID_W
    assert rows >= 2 * WIN_R and rows % NA_ROWS_PER_STEP == 0
    n = bsz * t
    tq = NA_ROWS_PER_STEP * GRID_W
    steps = t // tq
    return pl.pallas_call(
        functools.partial(_na_kernel, rows_total=rows),
        grid=(bsz, steps),
        in_specs=[
            pl.BlockSpec((1, tq, PCHUNK), lambda b, i: (CH_NA_Q, b * steps + i, 0)),
            pl.BlockSpec((1, t, PCHUNK), lambda b, i: (CH_NA_K, b, 0)),
            pl.BlockSpec((1, t, PCHUNK), lambda b, i: (CH_NA_V, b, 0)),
            _resident((NA_VARIANTS, NA_HEADS // 2, 2 * GRID_W, NA_BAND)),
        ],
        out_specs=pl.BlockSpec((tq, NA_WIDTH), lambda b, i: (b * steps + i, 0)),
        out_shape=jax.ShapeDtypeStruct((n, NA_WIDTH), BF16),
        compiler_params=_cparams(("arbitrary", "arbitrary")),
        name="neighborhood_attn",
    )(p3, p3, p3, bias_tab)


def _rec_kernel(*refs, mode, reverse, final, chunk, block):
    it = iter(refs)
    q_ref = next(it)
    if mode == "gla":
        k_ref, v_ref, lr_ref, uph_ref, upl_ref, gbias_ref = (next(it) for _ in range(5 + 1))
    else:
        f_ref, v_ref, lb_ref = (next(it) for _ in range(3))
    tri_ref = next(it)
    if final:
        prev_ref, gate_ref, nw_ref = (next(it) for _ in range(3))
    o_ref = next(it)
    st_ref, st_old, kbuf, wbuf, qbuf = (next(it) for _ in range(5))

    c, s = chunk, block
    nb = c // s
    width = N_REC_HEADS * HEAD_LANES
    n_chunks = REC_TOKENS_PER_STEP // c

    @pl.when(pl.program_id(1) == 0)
    def _():
        st_ref[...] = jnp.zeros_like(st_ref)

    ii = lax.broadcasted_iota(jnp.int32, (c, c), 0)
    jj = lax.broadcasted_iota(jnp.int32, (c, c), 1)
    bi, bj = ii // s, jj // s
    gap = (bj - bi) if reverse else (bi - bj)
    seen = (jj >= ii) if reverse else (jj <= ii)
    cls = jnp.where(gap == 0, jnp.where(seen, 0, -1), jnp.where(gap > 0, gap, -1))
    row_id = lax.broadcasted_iota(jnp.int32, (c, 1), 0)
    col_id = lax.broadcasted_iota(jnp.int32, (1, c), 1)

    def prev_block(i, steps=1):
        return i + steps if reverse else i - steps

    def per_block(vals):
        return jnp.concatenate([jnp.broadcast_to(x, (s, width)) for x in vals], axis=0)

    def chunk_inputs(rows):
        if mode == "gla":
            q = q_ref[0, rows, :].astype(F32)
            k = k_ref[0, rows, :].astype(F32)
            lr = lr_ref[rows, :]
            z = _dot(lr, uph_ref[...]) + _dot(lr, upl_ref[...]) + gbias_ref[...]
            g = (jnp.minimum(z, 0.0) - jnp.log(1.0 + jnp.exp(-jnp.abs(z)))) * (1.0 / GLA_TAU)
        else:
            z = f_ref[0, rows, :].astype(F32)
            sig, sig_neg = _sigmoid_pair(z)
            lb = lb_ref[...]
            g = jnp.log(jnp.maximum(lb + (1.0 - lb) * sig, F_FLOOR))
            k = (1.0 - lb) * sig_neg
            q = _silu(q_ref[0, rows, :].astype(F32))
        return q, k, g

    def chunk_factors(q, k, w, exact):
        w3 = w.reshape(nb, s, width)
        last = 0 if reverse else s - 1
        wb = [w3[i, last:last + 1, :] for i in range(nb)]
        order = list(range(nb - 1, -1, -1)) if reverse else list(range(nb))
        beta = [None] * nb
        run = jnp.zeros((1, width), F32)
        for i in order:
            beta[i] = run
            run = run + wb[i]
        total = run
        ewb = [jnp.exp(x) for x in wb]
        qd = q * jnp.exp(w)
        ks = k * jnp.exp(per_block(wb) - w)
        kd16 = None if exact else (k * jnp.exp(-w)).astype(BF16)
        one = jnp.ones((1, width), F32)
        lhs, fac = [qd.astype(BF16)], [one] * nb
        for dgap in range(2, nb):
            fac = [fac[i] * (ewb[prev_block(i, dgap - 1)] if 0 <= prev_block(i, dgap - 1) < nb else one)
                   for i in range(nb)]
            lhs.append((qd * per_block(fac)).astype(BF16))
        ks16 = ks.astype(BF16)
        if nb == 1:
            return lhs, kd16, ks16, lhs[0], ks16, jnp.exp(total)
        qe16 = (qd * jnp.exp(per_block(beta))).astype(BF16)
        ke16 = (ks * jnp.exp(per_block([total - beta[i] - wb[i] for i in range(nb)]))).astype(BF16)
        return lhs, kd16, ks16, qe16, ke16, jnp.exp(total)

    def head_scores(h, lhs, kd16, ks16, sd):
        hs = slice(h * HEAD_LANES, (h + 1) * HEAD_LANES)
        if sd is None:
            sd = _dot_nt(lhs[0][:, hs], kd16[:, hs])
        a = jnp.where(cls == 0, sd, 0.0)
        if nb > 1:
            so = _dot_nt(jnp.concatenate([x[:, hs] for x in lhs], axis=0), ks16[:, hs])
            for dgap in range(1, nb):
                a = jnp.where(cls == dgap, so[(dgap - 1) * c:dgap * c], a)
        return a.astype(BF16)

    def emit(h, rows, o):
        hs = slice(h * HEAD_LANES, (h + 1) * HEAD_LANES)
        if final:
            tot = prev_ref[rows, hs] + o
            ms = jnp.mean(tot * tot, axis=-1, keepdims=True)
            y = tot * lax.rsqrt(ms + RMS_EPS) * nw_ref[...]
            o_ref[rows, hs] = (y * _silu(gate_ref[0, rows, hs].astype(F32))).astype(o_ref.dtype)
        else:
            o_ref[rows, hs] = o

    def exact_same_block_scores(u, h):
        hs = slice(h * HEAD_LANES, (h + 1) * HEAD_LANES)

        def col_body(jg, acc):
            j0 = pl.multiple_of(jg * SUBLANES, SUBLANES)
            k8 = kbuf[u, pl.ds(j0, SUBLANES), hs]
            w8 = wbuf[u, pl.ds(j0, SUBLANES), hs]
            for jr in range(SUBLANES):
                j = j0 + jr
                same = (row_id // s) == (j // s)
                valid = same & ((row_id <= j) if reverse else (row_id >= j))
                e = jnp.exp(jnp.where(valid, wbuf[u, :, hs] - w8[jr:jr + 1], 0.0))
                t = jnp.sum(qbuf[u, :, hs] * k8[jr:jr + 1] * e, axis=-1, keepdims=True)
                acc = jnp.where(valid & (col_id == j), t, acc)
            return acc

        return lax.fori_loop(0, c // SUBLANES, col_body, jnp.zeros((c, c), F32))

    def group_body(gi, carry):
        all_rows = []
        for u in range(REC_UNROLL):
            ci = gi * REC_UNROLL + u
            cc = (n_chunks - 1 - ci) if reverse else ci
            all_rows.append(pl.ds(pl.multiple_of(cc * c, c), c))
        ws, facs = [], []
        for u, rows in enumerate(all_rows):
            q, k, g = chunk_inputs(rows)
            g_hi, g_lo = _split_bf16(g)
            w2 = _dot(tri_ref[...], jnp.concatenate([g_hi, g_lo], axis=1))
            w = w2[:, :width] + w2[:, width:]
            ws.append(w)
            facs.append(chunk_factors(q, k, w, exact=False))
            qbuf[u] = q
            kbuf[u] = k
            wbuf[u] = w
        a16 = [[head_scores(h, f[0], f[1], f[2], None) for h in range(N_REC_HEADS)] for f in facs]
        for h in range(N_REC_HEADS):
            hs = slice(h * HEAD_LANES, (h + 1) * HEAD_LANES)
            st = st_ref[h]
            for u, rows in enumerate(all_rows):
                _, _, _, qe16, ke16, decay = facs[u]
                v16 = v_ref[0, rows, hs]
                st_old[u, h] = st
                emit(h, rows, _dot(a16[u][h], v16) + _dot_nt(qe16[:, hs], st.astype(BF16)))
                st = st * decay[:, hs] + _dot_tn(v16, ke16[:, hs])
            st_ref[h] = st

        w_min = functools.reduce(jnp.minimum, ws)

        @pl.when(jnp.min(w_min) < -DECAY_SPAN_LIMIT)
        def _():
            for u, rows in enumerate(all_rows):
                lhs, _, ks16, qe16, _, _ = chunk_factors(qbuf[u], kbuf[u], wbuf[u], exact=True)
                for h in range(N_REC_HEADS):
                    hs = slice(h * HEAD_LANES, (h + 1) * HEAD_LANES)
                    a = head_scores(h, lhs, None, ks16, exact_same_block_scores(u, h))
                    emit(h, rows, _dot(a, v_ref[0, rows, hs]) + _dot_nt(qe16[:, hs], st_old[u, h].astype(BF16)))

        return carry

    lax.fori_loop(0, n_chunks // REC_UNROLL, group_body, 0)


def _scan_mask(chunk, block, reverse):
    i = np.arange(chunk)
    same = (i[:, None] // block) == (i[None, :] // block)
    seen = (i[None, :] >= i[:, None]) if reverse else (i[None, :] <= i[:, None])
    return jnp.asarray((same & seen).astype(np.float32), BF16)


def _recurrent_pass(mode, reverse, final, chunk, block, p3, bsz, t, extra, prev=None, gate_chunk=None, norm_w=None):
    n = bsz * t
    tb = REC_TOKENS_PER_STEP
    nblk = t // tb
    width = N_REC_HEADS * HEAD_LANES

    def tok(b, i):
        return b * nblk + ((nblk - 1 - i) if reverse else i)

    def chunk_spec(ch):
        return pl.BlockSpec((1, tb, PCHUNK), lambda b, i: (ch, tok(b, i), 0))

    tok_spec = lambda w: pl.BlockSpec((tb, w), lambda b, i: (tok(b, i), 0))
    operands, specs = [], []
    if mode == "gla":
        lr, up_hi, up_lo, gbias = extra
        operands += [p3, p3, p3, lr, up_hi, up_lo, gbias]
        specs += [chunk_spec(CH_GLA_Q), chunk_spec(CH_GLA_K), chunk_spec(CH_GLA_V), tok_spec(LR_PAD),
                  _resident(up_hi.shape), _resident(up_lo.shape), _resident(gbias.shape)]
        gate_ch = CH_GLA_G
    else:
        (lb,) = extra
        operands += [p3, p3, p3, lb]
        specs += [chunk_spec(CH_HG_Q), chunk_spec(CH_HG_FB if reverse else CH_HG_FF), chunk_spec(CH_HG_I),
                  _resident(lb.shape)]
        gate_ch = CH_HG_G
    tri = _scan_mask(chunk, block, reverse)
    operands.append(tri)
    specs.append(_resident(tri.shape))
    if final:
        operands += [prev, p3, norm_w]
        specs += [tok_spec(width), chunk_spec(gate_ch), _resident(norm_w.shape)]
    out_dtype = BF16 if final else F32
    return pl.pallas_call(
        functools.partial(_rec_kernel, mode=mode, reverse=reverse, final=final, chunk=chunk, block=block),
        grid=(bsz, nblk),
        in_specs=specs,
        out_specs=tok_spec(width),
        out_shape=jax.ShapeDtypeStruct((n, width), out_dtype),
        scratch_shapes=[
            pltpu.VMEM((N_REC_HEADS, HEAD_LANES, HEAD_LANES), F32),
            pltpu.VMEM((REC_UNROLL, N_REC_HEADS, HEAD_LANES, HEAD_LANES), F32),
            pltpu.VMEM((REC_UNROLL, chunk, width), F32),
            pltpu.VMEM((REC_UNROLL, chunk, width), F32),
            pltpu.VMEM((REC_UNROLL, chunk, width), F32),
        ],
        compiler_params=_cparams(("arbitrary", "arbitrary")),
        name=f"{mode}_{'bwd' if reverse else 'fwd'}",
    )(*operands)


GLA_CHUNK, GLA_BLOCK = 128, 128
HGRN_CHUNK, HGRN_BLOCK = 64, 16


def _gla_branch(p3, lr, up_pads, gbias_pads, norm_w, bsz, t):
    fwd = _recurrent_pass("gla", False, False, GLA_CHUNK, GLA_BLOCK, p3, bsz, t,
                          (lr, up_pads[0][0], up_pads[0][1], gbias_pads[0]))
    return _recurrent_pass("gla", True, True, GLA_CHUNK, GLA_BLOCK, p3, bsz, t,
                           (lr, up_pads[1][0], up_pads[1][1], gbias_pads[1]), prev=fwd, norm_w=norm_w)


def _hgrn_branch(p3, lbs, norm_w, bsz, t):
    fwd = _recurrent_pass("hgrn", False, False, HGRN_CHUNK, HGRN_BLOCK, p3, bsz, t, (lbs[0],))
    return _recurrent_pass("hgrn", True, True, HGRN_CHUNK, HGRN_BLOCK, p3, bsz, t, (lbs[1],),
                           prev=fwd, norm_w=norm_w)


MLP_ROWS = 256


def _mlp_kernel(x_ref, ona_ref, ogla_ref, ohg_ref, gates_ref, mod_ref, wna_ref, wgla_ref, whg_ref, wout_ref,
                n2_ref, wg_ref, wu_ref, wd_ref, fn_ref, o_ref, *, last):
    d = D_MODEL

    def gate(i):
        return jnp.concatenate([gates_ref[2 * i].astype(F32), gates_ref[2 * i + 1].astype(F32)], axis=1)

    merged = (_sigmoid_pair(gate(0))[0] * _dot(ona_ref[...], wna_ref[...])
              + _sigmoid_pair(gate(1))[0] * _dot(ogla_ref[...], wgla_ref[...])
              + _sigmoid_pair(gate(2))[0] * _dot(ohg_ref[...], whg_ref[...]))
    gate1 = mod_ref[0, :, 2 * d:3 * d]
    shift2 = mod_ref[0, :, 3 * d:4 * d]
    scale2 = mod_ref[0, :, 4 * d:5 * d]
    gate2 = mod_ref[0, :, 5 * d:6 * d]
    x1 = x_ref[...] + gate1 * _dot(merged.astype(BF16), wout_ref[...])
    h = _rmsnorm_mod(x1, n2_ref[...], scale2, shift2).astype(BF16)
    act = (_silu(_dot(h, wg_ref[...])) * _dot(h, wu_ref[...])).astype(BF16)
    x2 = x1 + gate2 * _dot(act, wd_ref[...])
    if last:
        ms = jnp.mean(x2 * x2, axis=-1, keepdims=True)
        x2 = x2 * lax.rsqrt(ms + RMS_EPS) * fn_ref[...]
    o_ref[...] = x2


def _merge_mlp(x2, o_na, o_gla, o_hg, p3, mod_l, wts, tokens_per_batch, last):
    n, d = x2.shape
    tm = MLP_ROWS
    per_batch = tokens_per_batch // tm
    row = lambda w: pl.BlockSpec((tm, w), lambda i: (i, 0))
    return pl.pallas_call(
        functools.partial(_mlp_kernel, last=last),
        grid=(n // tm,),
        in_specs=[
            row(d), row(NA_WIDTH), row(NA_WIDTH), row(NA_WIDTH),
            pl.BlockSpec((N_GATE_CHUNKS, tm, PCHUNK), lambda i: (CH_GATES // N_GATE_CHUNKS, i, 0)),
            pl.BlockSpec((1, 1, N_MOD * d), lambda i: (i // per_batch, 0, 0)),
        ] + [_resident(w.shape) for w in wts],
        out_specs=row(d),
        out_shape=jax.ShapeDtypeStruct((n, d), F32),
        compiler_params=_cparams(("arbitrary",)),
        name="merge_mlp",
    )(x2, o_na, o_gla, o_hg, p3, mod_l, *wts)


def _pad_heads(w, heads, dk):
    lead = w.shape[:-1]
    w = w.reshape(*lead, heads, dk)
    w = jnp.pad(w, [(0, 0)] * len(lead) + [(0, 0), (0, HEAD_LANES - dk)])
    return w.reshape(*lead, heads * HEAD_LANES)


def _layout_w_in(w):
    sizes = (NA_WIDTH, NA_WIDTH, NA_WIDTH, 256, 256, 512, 512, 2 * GLA_RANK, 512, 1024, 512, 512, 3 * D_MODEL)
    (na_q, na_k, na_v, gq, gk, gv, gg, glr, hq, hf, hi, hg, gates) = jnp.split(w, list(np.cumsum(sizes)[:-1]), axis=1)
    na_q = na_q * (NA_HEAD_DIM ** -0.5)
    gq = _pad_heads(gq * (GLA_DK ** -0.5), GLA_HEADS, GLA_DK)
    gk = _pad_heads(gk, GLA_HEADS, GLA_DK)
    cols = jnp.concatenate([na_q, na_k, na_v, gq, gk, gv, gg, hq, hf, hi, hg, gates], axis=1)
    d = w.shape[0]
    chunks = cols.reshape(d, N_PCHUNKS, PCHUNK).transpose(1, 0, 2).astype(BF16)
    w_lr = jnp.pad(glr, ((0, 0), (0, LR_PAD - 2 * GLA_RANK))).astype(BF16)
    return chunks, w_lr


def kernel(x, c, w_ada, b_ada, norm1_w, w_in, na_rpb, gla_lr_up, gla_lr_bias, gla_norm_w, hgrn_lb_logits, hgrn_norm_w, w_proj_na, w_proj_gla, w_proj_hgrn, w_out, norm2_w, w_ffn_gate, w_ffn_up, w_ffn_down, final_norm_w):
    bsz, t, d = x.shape
    depth = w_in.shape[0]
    assert d == D_MODEL and t % REC_TOKENS_PER_STEP == 0 and t % INPROJ_ROWS == 0
    n = bsz * t

    mods = _modulation(c, w_ada, b_ada).reshape(depth, bsz, 1, N_MOD * d)
    lb_p = jax.nn.softmax(hgrn_lb_logits.astype(F32), axis=0)
    lb_all = jnp.clip(jnp.cumsum(lb_p, axis=0) - lb_p[0], 0.0, 1.0)

    xf = x.reshape(n, d)
    for l in range(depth):
        w_chunks, w_lr = _layout_w_in(w_in[l])
        p3, lr = _in_projection(xf, mods[l], norm1_w[l].reshape(1, d), w_chunks, w_lr, t)

        o_na = _neighborhood_attention(p3, _na_bias_table(na_rpb[l], t // GRID_W), bsz, t)

        up_pads, gbias_pads = [], []
        for s in range(2):
            up = jnp.zeros((LR_PAD, N_REC_HEADS * HEAD_LANES), F32)
            up = up.at[s * GLA_RANK:(s + 1) * GLA_RANK].set(_pad_heads(gla_lr_up[l, s], GLA_HEADS, GLA_DK))
            hi = up.astype(BF16)
            up_pads.append((hi, (up - hi.astype(F32)).astype(BF16)))
            gbias_pads.append(_pad_heads(gla_lr_bias[l, s].reshape(1, -1), GLA_HEADS, GLA_DK))
        o_gla = _gla_branch(p3, lr, up_pads, gbias_pads, gla_norm_w[l].reshape(1, -1), bsz, t)

        lbs = [lb_all[l, s * HGRN_KEY_WIDTH:(s + 1) * HGRN_KEY_WIDTH].reshape(1, -1) for s in range(2)]
        o_hg = _hgrn_branch(p3, lbs, hgrn_norm_w[l].reshape(1, -1), bsz, t)

        wts = [w_proj_na[l].astype(BF16), w_proj_gla[l].astype(BF16), w_proj_hgrn[l].astype(BF16),
               w_out[l].astype(BF16), norm2_w[l].reshape(1, d), w_ffn_gate[l].astype(BF16),
               w_ffn_up[l].astype(BF16), w_ffn_down[l].astype(BF16), final_norm_w.reshape(1, d)]
        xf = _merge_mlp(xf, o_na, o_gla, o_hg, p3, mods[l], wts, t, last=(l == depth - 1))
    return xf.reshape(bsz, t, d)
```

```python
import functools

import jax
import jax.numpy as jnp
import numpy as np
from jax import lax
from jax.experimental import pallas as pl
from jax.experimental.pallas import tpu as pltpu

F32 = jnp.float32
BF16 = jnp.bfloat16

D_MODEL = 1024
GRID_W = 64
WIN_R = 8
WIN_C = 16
NA_HEADS = 8
NA_HEAD_DIM = 64
NA_WIDTH = 512
GLA_HEADS = 4
GLA_DK = 64
GLA_RANK = 16
GLA_TAU = 16.0
HGRN_KEY_WIDTH = 512
N_MOD = 6
FFN_HIDDEN = 2816
RMS_EPS = 1e-6
NEG_INF = -1e30
F_FLOOR = 1e-30

PCHUNK = 512
(CH_NA_Q, CH_NA_K, CH_NA_V, CH_GLA_Q, CH_GLA_K, CH_GLA_V, CH_GLA_G,
 CH_HG_Q, CH_HG_FF, CH_HG_FB, CH_HG_I, CH_HG_G, CH_GATES) = range(13)
N_GATE_CHUNKS = 6
N_PCHUNKS = CH_GATES + N_GATE_CHUNKS
LR_PAD = 128
HEAD_LANES = 128
SUBLANES = 8
N_REC_HEADS = 4

VMEM_LIMIT_BYTES = 56 * 1024 * 1024

REC_TOKENS_PER_STEP = 512
REC_UNROLL = 4
DECAY_SPAN_LIMIT = 60.0


def _cparams(sem):
    return pltpu.CompilerParams(dimension_semantics=sem, vmem_limit_bytes=VMEM_LIMIT_BYTES)


def _resident(shape):
    nd = len(shape)
    return pl.BlockSpec(shape, lambda *_: (0,) * nd, pipeline_mode=pl.Buffered(1))


def _dot(a, b):
    return jnp.dot(a, b, preferred_element_type=F32)


def _dot_nt(a, b):
    return lax.dot_general(a, b, (((1,), (1,)), ((), ())), preferred_element_type=F32)


def _dot_tn(a, b):
    return lax.dot_general(a, b, (((0,), (0,)), ((), ())), preferred_element_type=F32)


def _sigmoid_pair(z):
    e = jnp.exp(-jnp.abs(z))
    r = 1.0 / (1.0 + e)
    er = e * r
    pos = z >= 0
    return jnp.where(pos, r, er), jnp.where(pos, er, r)


def _silu(z):
    return z * _sigmoid_pair(z)[0]


def _split_bf16(a):
    hi = a.astype(BF16)
    lo = (a - hi.astype(F32)).astype(BF16)
    return hi, lo


MOD_COLS = 1536


def _mod_kernel(c_ref, w_ref, b_ref, o_ref):
    c_act = _silu(c_ref[...])
    c_hi, c_lo = _split_bf16(c_act)
    w_hi, w_lo = _split_bf16(w_ref[0])
    acc = _dot(c_hi, w_hi) + _dot(c_lo, w_hi) + _dot(c_hi, w_lo)
    o_ref[0] = acc + b_ref[0]


def _modulation(c, w_ada, b_ada):
    depth, d, width = w_ada.shape
    bsz = c.shape[0]
    rows = 8
    c_pad = jnp.zeros((rows, d), F32).at[:bsz].set(c)
    out = pl.pallas_call(
        _mod_kernel,
        grid=(depth, width // MOD_COLS),
        in_specs=[
            pl.BlockSpec((rows, d), lambda l, j: (0, 0)),
            pl.BlockSpec((1, d, MOD_COLS), lambda l, j: (l, 0, j)),
            pl.BlockSpec((1, 1, MOD_COLS), lambda l, j: (l, 0, j)),
        ],
        out_specs=pl.BlockSpec((1, rows, MOD_COLS), lambda l, j: (l, 0, j)),
        out_shape=jax.ShapeDtypeStruct((depth, rows, width), F32),
        compiler_params=_cparams(("arbitrary", "arbitrary")),
        name="adaln_mod",
    )(c_pad, w_ada, b_ada.reshape(depth, 1, width))
    return out[:, :bsz]


INPROJ_ROWS = 512
INPROJ_UNROLL = 3


def _rmsnorm_mod(x, w, scale, shift):
    ms = jnp.mean(x * x, axis=-1, keepdims=True)
    y = x * lax.rsqrt(ms + RMS_EPS) * w
    return y * (1.0 + scale) + shift


def _inproj_kernel(x_ref, mod_ref, nw_ref, w_ref, wlr_ref, p_ref, lr_ref, h_ref):
    d = D_MODEL
    shift = mod_ref[0, :, 0:d]
    scale = mod_ref[0, :, d:2 * d]
    h_ref[...] = _rmsnorm_mod(x_ref[...], nw_ref[...], scale, shift).astype(BF16)

    def body(j, carry):
        p_ref[j] = _dot(h_ref[...], w_ref[j]).astype(BF16)
        return carry

    lax.fori_loop(0, N_PCHUNKS, body, 0, unroll=INPROJ_UNROLL)
    lr_ref[...] = _dot(h_ref[...], wlr_ref[...]).astype(BF16)


def _in_projection(x2, mod_l, norm_w, w_chunks, w_lr, tokens_per_batch):
    n, d = x2.shape
    tm = INPROJ_ROWS
    per_batch = tokens_per_batch // tm
    return pl.pallas_call(
        _inproj_kernel,
        grid=(n // tm,),
        in_specs=[
            pl.BlockSpec((tm, d), lambda i: (i, 0)),
            pl.BlockSpec((1, 1, N_MOD * d), lambda i: (i // per_batch, 0, 0)),
            _resident((1, d)),
            _resident((N_PCHUNKS, d, PCHUNK)),
            _resident((d, LR_PAD)),
        ],
        out_specs=[
            pl.BlockSpec((N_PCHUNKS, tm, PCHUNK), lambda i: (0, i, 0)),
            pl.BlockSpec((tm, LR_PAD), lambda i: (i, 0)),
        ],
        out_shape=[
            jax.ShapeDtypeStruct((N_PCHUNKS, n, PCHUNK), BF16),
            jax.ShapeDtypeStruct((n, LR_PAD), BF16),
        ],
        scratch_shapes=[pltpu.VMEM((tm, d), BF16)],
        compiler_params=_cparams(("arbitrary",)),
        name="in_proj",
    )(x2, mod_l, norm_w, w_chunks, w_lr)


NA_ROWS_PER_STEP = 8
NA_ROW_UNROLL = 2
NA_BAND = WIN_R * GRID_W
NA_VARIANTS = 8


def _na_kernel(q_ref, k_ref, v_ref, bias_ref, o_ref, *, rows_total):
    step = pl.program_id(1)
    lane = lax.broadcasted_iota(jnp.int32, (GRID_W, HEAD_LANES), 1)
    low = lane < NA_HEAD_DIM

    def row_group(gi, carry):
        items = []
        for u in range(NA_ROW_UNROLL):
            rr = gi * NA_ROW_UNROLL + u
            r = step * NA_ROWS_PER_STEP + rr
            row_start = jnp.clip(r - WIN_R // 2, 0, rows_total - WIN_R)
            variant = jnp.where(r < WIN_R // 2, r,
                                jnp.where(r > rows_total - WIN_R // 2, r - (rows_total - WIN_R), WIN_R // 2))
            k0 = pl.multiple_of(row_start * GRID_W, GRID_W)
            q0 = pl.multiple_of(rr * GRID_W, GRID_W)
            for hp in range(NA_HEADS // 2):
                items.append((q0, k0, variant, hp, slice(hp * HEAD_LANES, (hp + 1) * HEAD_LANES)))
        scores = []
        for q0, k0, variant, hp, hs in items:
            qp = q_ref[0, pl.ds(q0, GRID_W), hs]
            zero = jnp.zeros_like(qp)
            q2 = jnp.concatenate([jnp.where(low, qp, zero), jnp.where(low, zero, qp)], axis=0)
            scores.append(_dot_nt(q2, k_ref[0, pl.ds(k0, NA_BAND), hs]) + bias_ref[variant, hp])
        probs = []
        for s in scores:
            e = jnp.exp(s - jnp.max(s, axis=-1, keepdims=True))
            probs.append((e.astype(BF16), jnp.sum(e, axis=-1, keepdims=True)))
        for (q0, k0, variant, hp, hs), (e16, denom) in zip(items, probs):
            o2 = _dot(e16, v_ref[0, pl.ds(k0, NA_BAND), hs]) / denom
            o_ref[pl.ds(q0, GRID_W), hs] = jnp.where(low, o2[:GRID_W], o2[GRID_W:]).astype(BF16)
        return carry

    lax.fori_loop(0, NA_ROWS_PER_STEP // NA_ROW_UNROLL, row_group, 0)


def _na_bias_table(rpb, rows):
    wr = min(WIN_R, rows)
    col = np.arange(GRID_W)
    col_start = np.clip(col - WIN_C // 2, 0, GRID_W - WIN_C)
    col_mask = (col[None, :] >= col_start[:, None]) & (col[None, :] < col_start[:, None] + WIN_C)
    col_off = np.clip(col[None, :] - col[:, None], -(WIN_C - 1), WIN_C - 1) + (WIN_C - 1)
    n_ro, n_co = 2 * WIN_R - 1, 2 * WIN_C - 1
    onehot = (col_off.reshape(-1)[None, :] == np.arange(n_co)[:, None]).astype(np.float32)
    by_col = jnp.dot(rpb.astype(F32).reshape(NA_HEADS * n_ro, n_co), onehot, precision=lax.Precision.HIGHEST)
    by_col = by_col.reshape(NA_HEADS, n_ro, GRID_W, GRID_W)
    by_col = jnp.where(col_mask[None, None], by_col, NEG_INF)
    rep_rows = list(range(WIN_R // 2)) + [WIN_R // 2] + [rows - WIN_R + v for v in range(WIN_R // 2 + 1, WIN_R)]
    tabs = []
    for r in rep_rows:
        row_start = int(np.clip(r - WIN_R // 2, 0, rows - wr))
        ro0 = row_start - r + (WIN_R - 1)
        b = by_col[:, ro0:ro0 + wr].transpose(0, 2, 1, 3)
        tabs.append(b.reshape(NA_HEADS // 2, 2 * GRID_W, wr * GRID_W))
    return jnp.stack(tabs)


def _neighborhood_attention(p3, bias_tab, bsz, t):
    rows = t // GRID_W
    assert rows >= 2 * WIN_R and rows % NA_ROWS_PER_STEP == 0
    n = bsz * t
    tq = NA_ROWS_PER_STEP * GRID_W
    steps = t // tq
    return pl.pallas_call(
        functools.partial(_na_kernel, rows_total=rows),
        grid=(bsz, steps),
        in_specs=[
            pl.BlockSpec((1, tq, PCHUNK), lambda b, i: (CH_NA_Q, b * steps + i, 0)),
            pl.BlockSpec((1, t, PCHUNK), lambda b, i: (CH_NA_K, b, 0)),
            pl.BlockSpec((1, t, PCHUNK), lambda b, i: (CH_NA_V, b, 0)),
            _resident((NA_VARIANTS, NA_HEADS // 2, 2 * GRID_W, NA_BAND)),
        ],
        out_specs=pl.BlockSpec((tq, NA_WIDTH), lambda b, i: (b * steps + i, 0)),
        out_shape=jax.ShapeDtypeStruct((n, NA_WIDTH), BF16),
        compiler_params=_cparams(("arbitrary", "arbitrary")),
        name="neighborhood_attn",
    )(p3, p3, p3, bias_tab)


def _rec_kernel(*refs, mode, reverse, final, chunk, block):
    it = iter(refs)
    q_ref = next(it)
    if mode == "gla":
        k_ref, v_ref, lr_ref, uph_ref, upl_ref, gbias_ref = (next(it) for _ in range(5 + 1))
    else:
        f_ref, v_ref, lb_ref = (next(it) for _ in range(3))
    tri_ref = next(it)
    if final:
        prev_ref, gate_ref, nw_ref = (next(it) for _ in range(3))
    o_ref = next(it)
    st_ref, st_old, kbuf, wbuf, qbuf = (next(it) for _ in range(5))

    c, s = chunk, block
    nb = c // s
    width = N_REC_HEADS * HEAD_LANES
    n_chunks = REC_TOKENS_PER_STEP // c

    @pl.when(pl.program_id(1) == 0)
    def _():
        st_ref[...] = jnp.zeros_like(st_ref)

    ii = lax.broadcasted_iota(jnp.int32, (c, c), 0)
    jj = lax.broadcasted_iota(jnp.int32, (c, c), 1)
    bi, bj = ii // s, jj // s
    gap = (bj - bi) if reverse else (bi - bj)
    seen = (jj >= ii) if reverse else (jj <= ii)
    cls = jnp.where(gap == 0, jnp.where(seen, 0, -1), jnp.where(gap > 0, gap, -1))
    row_id = lax.broadcasted_iota(jnp.int32, (c, 1), 0)
    col_id = lax.broadcasted_iota(jnp.int32, (1, c), 1)

    def prev_block(i, steps=1):
        return i + steps if reverse else i - steps

    def per_block(vals):
        return jnp.concatenate([jnp.broadcast_to(x, (s, width)) for x in vals], axis=0)

    def chunk_inputs(rows):
        if mode == "gla":
            q = q_ref[0, rows, :].astype(F32)
            k = k_ref[0, rows, :].astype(F32)
            lr = lr_ref[rows, :]
            z = _dot(lr, uph_ref[...]) + _dot(lr, upl_ref[...]) + gbias_ref[...]
            g = (jnp.minimum(z, 0.0) - jnp.log(1.0 + jnp.exp(-jnp.abs(z)))) * (1.0 / GLA_TAU)
        else:
            z = f_ref[0, rows, :].astype(F32)
            sig, sig_neg = _sigmoid_pair(z)
            lb = lb_ref[...]
            g = jnp.log(jnp.maximum(lb + (1.0 - lb) * sig, F_FLOOR))
            k = (1.0 - lb) * sig_neg
            q = _silu(q_ref[0, rows, :].astype(F32))
        return q, k, g

    def chunk_factors(q, k, w, exact):
        w3 = w.reshape(nb, s, width)
        last = 0 if reverse else s - 1
        wb = [w3[i, last:last + 1, :] for i in range(nb)]
        order = list(range(nb - 1, -1, -1)) if reverse else list(range(nb))
        beta = [None] * nb
        run = jnp.zeros((1, width), F32)
        for i in order:
            beta[i] = run
            run = run + wb[i]
        total = run
        ewb = [jnp.exp(x) for x in wb]
        qd = q * jnp.exp(w)
        ks = k * jnp.exp(per_block(wb) - w)
        kd16 = None if exact else (k * jnp.exp(-w)).astype(BF16)
        one = jnp.ones((1, width), F32)
        lhs, fac = [qd.astype(BF16)], [one] * nb
        for dgap in range(2, nb):
            fac = [fac[i] * (ewb[prev_block(i, dgap - 1)] if 0 <= prev_block(i, dgap - 1) < nb else one)
                   for i in range(nb)]
            lhs.append((qd * per_block(fac)).astype(BF16))
        ks16 = ks.astype(BF16)
        if nb == 1:
            return lhs, kd16, ks16, lhs[0], ks16, jnp.exp(total)
        qe16 = (qd * jnp.exp(per_block(beta))).astype(BF16)
        ke16 = (ks * jnp.exp(per_block([total - beta[i] - wb[i] for i in range(nb)]))).astype(BF16)
        return lhs, kd16, ks16, qe16, ke16, jnp.exp(total)

    def head_scores(h, lhs, kd16, ks16, sd):
        hs = slice(h * HEAD_LANES, (h + 1) * HEAD_LANES)
        if sd is None:
            sd = _dot_nt(lhs[0][:, hs], kd16[:, hs])
        a = jnp.where(cls == 0, sd, 0.0)
        if nb > 1:
            so = _dot_nt(jnp.concatenate([x[:, hs] for x in lhs], axis=0), ks16[:, hs])
            for dgap in range(1, nb):
                a = jnp.where(cls == dgap, so[(dgap - 1) * c:dgap * c], a)
        return a.astype(BF16)

    def emit(h, rows, o):
        hs = slice(h * HEAD_LANES, (h + 1) * HEAD_LANES)
        if final:
            tot = prev_ref[rows, hs] + o
            ms = jnp.mean(tot * tot, axis=-1, keepdims=True)
            y = tot * lax.rsqrt(ms + RMS_EPS) * nw_ref[...]
            o_ref[rows, hs] = (y * _silu(gate_ref[0, rows, hs].astype(F32))).astype(o_ref.dtype)
        else:
            o_ref[rows, hs] = o

    def exact_same_block_scores(u, h):
        hs = slice(h * HEAD_LANES, (h + 1) * HEAD_LANES)

        def col_body(jg, acc):
            j0 = pl.multiple_of(jg * SUBLANES, SUBLANES)
            k8 = kbuf[u, pl.ds(j0, SUBLANES), hs]
            w8 = wbuf[u, pl.ds(j0, SUBLANES), hs]
            for jr in range(SUBLANES):
                j = j0 + jr
                same = (row_id // s) == (j // s)
                valid = same & ((row_id <= j) if reverse else (row_id >= j))
                e = jnp.exp(jnp.where(valid, wbuf[u, :, hs] - w8[jr:jr + 1], 0.0))
                t = jnp.sum(qbuf[u, :, hs] * k8[jr:jr + 1] * e, axis=-1, keepdims=True)
                acc = jnp.where(valid & (col_id == j), t, acc)
            return acc

        return lax.fori_loop(0, c // SUBLANES, col_body, jnp.zeros((c, c), F32))

    def group_body(gi, carry):
        all_rows = []
        for u in range(REC_UNROLL):
            ci = gi * REC_UNROLL + u
            cc = (n_chunks - 1 - ci) if reverse else ci
            all_rows.append(pl.ds(pl.multiple_of(cc * c, c), c))
        ws, facs = [], []
        for u, rows in enumerate(all_rows):
            q, k, g = chunk_inputs(rows)
            g_hi, g_lo = _split_bf16(g)
            w2 = _dot(tri_ref[...], jnp.concatenate([g_hi, g_lo], axis=1))
            w = w2[:, :width] + w2[:, width:]
            ws.append(w)
            facs.append(chunk_factors(q, k, w, exact=False))
            qbuf[u] = q
            kbuf[u] = k
            wbuf[u] = w
        a16 = [[head_scores(h, f[0], f[1], f[2], None) for h in range(N_REC_HEADS)] for f in facs]
        for h in range(N_REC_HEADS):
            hs = slice(h * HEAD_LANES, (h + 1) * HEAD_LANES)
            st = st_ref[h]
            for u, rows in enumerate(all_rows):
                _, _, _, qe16, ke16, decay = facs[u]
                v16 = v_ref[0, rows, hs]
                st_old[u, h] = st
                emit(h, rows, _dot(a16[u][h], v16) + _dot_nt(qe16[:, hs], st.astype(BF16)))
                st = st * decay[:, hs] + _dot_tn(v16, ke16[:, hs])
            st_ref[h] = st

        w_min = functools.reduce(jnp.minimum, ws)

        @pl.when(jnp.min(w_min) < -DECAY_SPAN_LIMIT)
        def _():
            for u, rows in enumerate(all_rows):
                lhs, _, ks16, qe16, _, _ = chunk_factors(qbuf[u], kbuf[u], wbuf[u], exact=True)
                for h in range(N_REC_HEADS):
                    hs = slice(h * HEAD_LANES, (h + 1) * HEAD_LANES)
                    a = head_scores(h, lhs, None, ks16, exact_same_block_scores(u, h))
                    emit(h, rows, _dot(a, v_ref[0, rows, hs]) + _dot_nt(qe16[:, hs], st_old[u, h].astype(BF16)))

        return carry

    lax.fori_loop(0, n_chunks // REC_UNROLL, group_body, 0)


def _scan_mask(chunk, block, reverse):
    i = np.arange(chunk)
    same = (i[:, None] // block) == (i[None, :] // block)
    seen = (i[None, :] >= i[:, None]) if reverse else (i[None, :] <= i[:, None])
    return jnp.asarray((same & seen).astype(np.float32), BF16)


def _recurrent_pass(mode, reverse, final, chunk, block, p3, bsz, t, extra, prev=None, gate_chunk=None, norm_w=None):
    n = bsz * t
    tb = REC_TOKENS_PER_STEP
    nblk = t // tb
    width = N_REC_HEADS * HEAD_LANES

    def tok(b, i):
        return b * nblk + ((nblk - 1 - i) if reverse else i)

    def chunk_spec(ch):
        return pl.BlockSpec((1, tb, PCHUNK), lambda b, i: (ch, tok(b, i), 0))

    tok_spec = lambda w: pl.BlockSpec((tb, w), lambda b, i: (tok(b, i), 0))
    operands, specs = [], []
    if mode == "gla":
        lr, up_hi, up_lo, gbias = extra
        operands += [p3, p3, p3, lr, up_hi, up_lo, gbias]
        specs += [chunk_spec(CH_GLA_Q), chunk_spec(CH_GLA_K), chunk_spec(CH_GLA_V), tok_spec(LR_PAD),
                  _resident(up_hi.shape), _resident(up_lo.shape), _resident(gbias.shape)]
        gate_ch = CH_GLA_G
    else:
        (lb,) = extra
        operands += [p3, p3, p3, lb]
        specs += [chunk_spec(CH_HG_Q), chunk_spec(CH_HG_FB if reverse else CH_HG_FF), chunk_spec(CH_HG_I),
                  _resident(lb.shape)]
        gate_ch = CH_HG_G
    tri = _scan_mask(chunk, block, reverse)
    operands.append(tri)
    specs.append(_resident(tri.shape))
    if final:
        operands += [prev, p3, norm_w]
        specs += [tok_spec(width), chunk_spec(gate_ch), _resident(norm_w.shape)]
    out_dtype = BF16 if final else F32
    return pl.pallas_call(
        functools.partial(_rec_kernel, mode=mode, reverse=reverse, final=final, chunk=chunk, block=block),
        grid=(bsz, nblk),
        in_specs=specs,
        out_specs=tok_spec(width),
        out_shape=jax.ShapeDtypeStruct((n, width), out_dtype),
        scratch_shapes=[
            pltpu.VMEM((N_REC_HEADS, HEAD_LANES, HEAD_LANES), F32),
            pltpu.VMEM((REC_UNROLL, N_REC_HEADS, HEAD_LANES, HEAD_LANES), F32),
            pltpu.VMEM((REC_UNROLL, chunk, width), F32),
            pltpu.VMEM((REC_UNROLL, chunk, width), F32),
            pltpu.VMEM((REC_UNROLL, chunk, width), F32),
        ],
        compiler_params=_cparams(("arbitrary", "arbitrary")),
        name=f"{mode}_{'bwd' if reverse else 'fwd'}",
    )(*operands)


GLA_CHUNK, GLA_BLOCK = 128, 128
HGRN_CHUNK, HGRN_BLOCK = 64, 16


def _gla_branch(p3, lr, up_pads, gbias_pads, norm_w, bsz, t):
    fwd = _recurrent_pass("gla", False, False, GLA_CHUNK, GLA_BLOCK, p3, bsz, t,
                          (lr, up_pads[0][0], up_pads[0][1], gbias_pads[0]))
    return _recurrent_pass("gla", True, True, GLA_CHUNK, GLA_BLOCK, p3, bsz, t,
                           (lr, up_pads[1][0], up_pads[1][1], gbias_pads[1]), prev=fwd, norm_w=norm_w)


def _hgrn_branch(p3, lbs, norm_w, bsz, t):
    fwd = _recurrent_pass("hgrn", False, False, HGRN_CHUNK, HGRN_BLOCK, p3, bsz, t, (lbs[0],))
    return _recurrent_pass("hgrn", True, True, HGRN_CHUNK, HGRN_BLOCK, p3, bsz, t, (lbs[1],),
                           prev=fwd, norm_w=norm_w)


MLP_ROWS = 512
MLP_GROUP_ROWS = 256


def _mlp_kernel(x_ref, ona_ref, ogla_ref, ohg_ref, gates_ref, mod_ref, wna_ref, wgla_ref, whg_ref, wout_ref,
                n2_ref, wg_ref, wu_ref, wd_ref, fn_ref, o_ref, *, last):
    d = D_MODEL
    gate1 = mod_ref[0, :, 2 * d:3 * d]
    shift2 = mod_ref[0, :, 3 * d:4 * d]
    scale2 = mod_ref[0, :, 4 * d:5 * d]
    gate2 = mod_ref[0, :, 5 * d:6 * d]
    groups = [slice(i * MLP_GROUP_ROWS, (i + 1) * MLP_GROUP_ROWS) for i in range(MLP_ROWS // MLP_GROUP_ROWS)]

    def gate(rs, i):
        g = jnp.concatenate([gates_ref[2 * i, rs, :].astype(F32), gates_ref[2 * i + 1, rs, :].astype(F32)], axis=1)
        return _sigmoid_pair(g)[0]

    merged = [(gate(rs, 0) * _dot(ona_ref[rs, :], wna_ref[...])
               + gate(rs, 1) * _dot(ogla_ref[rs, :], wgla_ref[...])
               + gate(rs, 2) * _dot(ohg_ref[rs, :], whg_ref[...])).astype(BF16) for rs in groups]
    x1 = [x_ref[rs, :] + gate1 * _dot(m, wout_ref[...]) for rs, m in zip(groups, merged)]
    h = [_rmsnorm_mod(x, n2_ref[...], scale2, shift2).astype(BF16) for x in x1]
    act = [(_silu(_dot(hh, wg_ref[...])) * _dot(hh, wu_ref[...])).astype(BF16) for hh in h]
    for rs, x, a in zip(groups, x1, act):
        x2 = x + gate2 * _dot(a, wd_ref[...])
        if last:
            ms = jnp.mean(x2 * x2, axis=-1, keepdims=True)
            x2 = x2 * lax.rsqrt(ms + RMS_EPS) * fn_ref[...]
        o_ref[rs, :] = x2


def _merge_mlp(x2, o_na, o_gla, o_hg, p3, mod_l, wts, tokens_per_batch, last):
    n, d = x2.shape
    tm = MLP_ROWS
    per_batch = tokens_per_batch // tm
    row = lambda w: pl.BlockSpec((tm, w), lambda i: (i, 0))
    return pl.pallas_call(
        functools.partial(_mlp_kernel, last=last),
        grid=(n // tm,),
        in_specs=[
            row(d), row(NA_WIDTH), row(NA_WIDTH), row(NA_WIDTH),
            pl.BlockSpec((N_GATE_CHUNKS, tm, PCHUNK), lambda i: (CH_GATES // N_GATE_CHUNKS, i, 0)),
            pl.BlockSpec((1, 1, N_MOD * d), lambda i: (i // per_batch, 0, 0)),
        ] + [_resident(w.shape) for w in wts],
        out_specs=row(d),
        out_shape=jax.ShapeDtypeStruct((n, d), F32),
        compiler_params=_cparams(("arbitrary",)),
        name="merge_mlp",
    )(x2, o_na, o_gla, o_hg, p3, mod_l, *wts)


def _pad_heads(w, heads, dk):
    lead = w.shape[:-1]
    w = w.reshape(*lead, heads, dk)
    w = jnp.pad(w, [(0, 0)] * len(lead) + [(0, 0), (0, HEAD_LANES - dk)])
    return w.reshape(*lead, heads * HEAD_LANES)


def _layout_w_in(w):
    sizes = (NA_WIDTH, NA_WIDTH, NA_WIDTH, 256, 256, 512, 512, 2 * GLA_RANK, 512, 1024, 512, 512, 3 * D_MODEL)
    (na_q, na_k, na_v, gq, gk, gv, gg, glr, hq, hf, hi, hg, gates) = jnp.split(w, list(np.cumsum(sizes)[:-1]), axis=1)
    na_q = na_q * (NA_HEAD_DIM ** -0.5)
    gq = _pad_heads(gq * (GLA_DK ** -0.5), GLA_HEADS, GLA_DK)
    gk = _pad_heads(gk, GLA_HEADS, GLA_DK)
    cols = jnp.concatenate([na_q, na_k, na_v, gq, gk, gv, gg, hq, hf, hi, hg, gates], axis=1)
    d = w.shape[0]
    chunks = cols.reshape(d, N_PCHUNKS, PCHUNK).transpose(1, 0, 2).astype(BF16)
    w_lr = jnp.pad(glr, ((0, 0), (0, LR_PAD - 2 * GLA_RANK))).astype(BF16)
    return chunks, w_lr


def kernel(x, c, w_ada, b_ada, norm1_w, w_in, na_rpb, gla_lr_up, gla_lr_bias, gla_norm_w, hgrn_lb_logits, hgrn_norm_w, w_proj_na, w_proj_gla, w_proj_hgrn, w_out, norm2_w, w_ffn_gate, w_ffn_up, w_ffn_down, final_norm_w):
    bsz, t, d = x.shape
    depth = w_in.shape[0]
    assert d == D_MODEL and t % REC_TOKENS_PER_STEP == 0 and t % INPROJ_ROWS == 0
    n = bsz * t

    mods = _modulation(c, w_ada, b_ada).reshape(depth, bsz, 1, N_MOD * d)
    lb_p = jax.nn.softmax(hgrn_lb_logits.astype(F32), axis=0)
    lb_all = jnp.clip(jnp.cumsum(lb_p, axis=0) - lb_p[0], 0.0, 1.0)

    xf = x.reshape(n, d)
    for l in range(depth):
        w_chunks, w_lr = _layout_w_in(w_in[l])
        p3, lr = _in_projection(xf, mods[l], norm1_w[l].reshape(1, d), w_chunks, w_lr, t)

        o_na = _neighborhood_attention(p3, _na_bias_table(na_rpb[l], t // GRID_W), bsz, t)

        up_pads, gbias_pads = [], []
        for s in range(2):
            up = jnp.zeros((LR_PAD, N_REC_HEADS * HEAD_LANES), F32)
            up = up.at[s * GLA_RANK:(s + 1) * GLA_RANK].set(_pad_heads(gla_lr_up[l, s], GLA_HEADS, GLA_DK))
            hi = up.astype(BF16)
            up_pads.append((hi, (up - hi.astype(F32)).astype(BF16)))
            gbias_pads.append(_pad_heads(gla_lr_bias[l, s].reshape(1, -1), GLA_HEADS, GLA_DK))
        o_gla = _gla_branch(p3, lr, up_pads, gbias_pads, gla_norm_w[l].reshape(1, -1), bsz, t)

        lbs = [lb_all[l, s * HGRN_KEY_WIDTH:(s + 1) * HGRN_KEY_WIDTH].reshape(1, -1) for s in range(2)]
        o_hg = _hgrn_branch(p3, lbs, hgrn_norm_w[l].reshape(1, -1), bsz, t)

        wts = [w_proj_na[l].astype(BF16), w_proj_gla[l].astype(BF16), w_proj_hgrn[l].astype(BF16),
               w_out[l].astype(BF16), norm2_w[l].reshape(1, d), w_ffn_gate[l].astype(BF16),
               w_ffn_up[l].astype(BF16), w_ffn_down[l].astype(BF16), final_norm_w.reshape(1, d)]
        xf = _merge_mlp(xf, o_na, o_gla, o_hg, p3, mods[l], wts, t, last=(l == depth - 1))
    return xf.reshape(bsz, t, d)
```

```python
import functools

import jax
import jax.numpy as jnp
import numpy as np
from jax import lax
from jax.experimental import pallas as pl
from jax.experimental.pallas import tpu as pltpu

F32 = jnp.float32
BF16 = jnp.bfloat16

D_MODEL = 1024
GRID_W = 64
WIN_R = 8
WIN_C = 16
NA_HEADS = 8
NA_HEAD_DIM = 64
NA_WIDTH = 512
GLA_HEADS = 4
GLA_DK = 64
GLA_RANK = 16
GLA_TAU = 16.0
HGRN_KEY_WIDTH = 512
N_MOD = 6
FFN_HIDDEN = 2816
RMS_EPS = 1e-6
NEG_INF = -1e30
F_FLOOR = 1e-30

PCHUNK = 512
N_GATE_CHUNKS = 6
(CH_NA_Q, CH_NA_K, CH_NA_V, CH_GLA_QK, CH_GLA_V, CH_GLA_G,
 CH_HG_Q, CH_HG_FF, CH_HG_FB, CH_HG_I, CH_HG_G) = range(N_GATE_CHUNKS, N_GATE_CHUNKS + 11)
N_PCHUNKS = N_GATE_CHUNKS + 11
LR_PAD = 128
HEAD_LANES = 128
SUBLANES = 8
N_REC_HEADS = 4

VMEM_LIMIT_BYTES = 56 * 1024 * 1024

REC_TOKENS_PER_STEP = 512
REC_UNROLL = 4
DECAY_SPAN_LIMIT = 60.0


def _cparams(sem):
    return pltpu.CompilerParams(dimension_semantics=sem, vmem_limit_bytes=VMEM_LIMIT_BYTES)


def _resident(shape):
    nd = len(shape)
    return pl.BlockSpec(shape, lambda *_: (0,) * nd, pipeline_mode=pl.Buffered(1))


def _layer_resident(shape, layer):
    nd = len(shape)
    return pl.BlockSpec((1,) + tuple(shape[1:]), lambda *_: (layer,) + (0,) * (nd - 1), pipeline_mode=pl.Buffered(1))


def _dot(a, b):
    return jnp.dot(a, b, preferred_element_type=F32)


def _dot_nt(a, b):
    return lax.dot_general(a, b, (((1,), (1,)), ((), ())), preferred_element_type=F32)


def _dot_tn(a, b):
    return lax.dot_general(a, b, (((0,), (0,)), ((), ())), preferred_element_type=F32)


def _sigmoid_pair(z):
    e = jnp.exp(-jnp.abs(z))
    r = 1.0 / (1.0 + e)
    er = e * r
    pos = z >= 0
    return jnp.where(pos, r, er), jnp.where(pos, er, r)


def _silu(z):
    return z * _sigmoid_pair(z)[0]


def _split_bf16(a):
    hi = a.astype(BF16)
    lo = (a - hi.astype(F32)).astype(BF16)
    return hi, lo


MOD_COLS = 1536


def _mod_kernel(c_ref, w_ref, b_ref, o_ref):
    c_act = _silu(c_ref[...])
    c_hi, c_lo = _split_bf16(c_act)
    w_hi, w_lo = _split_bf16(w_ref[0])
    acc = _dot(c_hi, w_hi) + _dot(c_lo, w_hi) + _dot(c_hi, w_lo)
    o_ref[0] = acc + b_ref[0]


def _modulation(c, w_ada, b_ada):
    depth, d, width = w_ada.shape
    bsz = c.shape[0]
    rows = 8
    c_pad = jnp.zeros((rows, d), F32).at[:bsz].set(c)
    out = pl.pallas_call(
        _mod_kernel,
        grid=(depth, width // MOD_COLS),
        in_specs=[
            pl.BlockSpec((rows, d), lambda l, j: (0, 0)),
            pl.BlockSpec((1, d, MOD_COLS), lambda l, j: (l, 0, j)),
            pl.BlockSpec((1, 1, MOD_COLS), lambda l, j: (l, 0, j)),
        ],
        out_specs=pl.BlockSpec((1, rows, MOD_COLS), lambda l, j: (l, 0, j)),
        out_shape=jax.ShapeDtypeStruct((depth, rows, width), F32),
        compiler_params=_cparams(("arbitrary", "arbitrary")),
        name="adaln_mod",
    )(c_pad, w_ada, b_ada.reshape(depth, 1, width))
    return out[:, :bsz]


INPROJ_ROWS = 512
INPROJ_UNROLL = 4
INPROJ_COLS = N_PCHUNKS * PCHUNK + LR_PAD


def _rmsnorm_mod(x, w, scale, shift):
    ms = jnp.mean(x * x, axis=-1, keepdims=True)
    y = x * lax.rsqrt(ms + RMS_EPS) * w
    return y * (1.0 + scale) + shift


def _inproj_kernel(x_ref, mod_ref, nw_ref, w_ref, p_ref, lr_ref, h_ref):
    d = D_MODEL
    shift = mod_ref[0, :, 0:d]
    scale = mod_ref[0, :, d:2 * d]
    h_ref[...] = _rmsnorm_mod(x_ref[...], nw_ref[0], scale, shift).astype(BF16)

    def body(j, carry):
        cols = pl.ds(pl.multiple_of(j * PCHUNK, PCHUNK), PCHUNK)
        p_ref[j] = _dot(h_ref[...], w_ref[0, :, cols]).astype(BF16)
        return carry

    n_looped = (N_PCHUNKS // INPROJ_UNROLL) * INPROJ_UNROLL
    lax.fori_loop(0, n_looped, body, 0, unroll=INPROJ_UNROLL)
    for j in range(n_looped, N_PCHUNKS):
        p_ref[j] = _dot(h_ref[...], w_ref[0, :, j * PCHUNK:(j + 1) * PCHUNK]).astype(BF16)
    lr_ref[...] = _dot(h_ref[...], w_ref[0, :, N_PCHUNKS * PCHUNK:]).astype(BF16)


def _in_projection(x2, mod_l, norm_w, w_all, layer, tokens_per_batch):
    n, d = x2.shape
    tm = INPROJ_ROWS
    per_batch = tokens_per_batch // tm
    return pl.pallas_call(
        _inproj_kernel,
        grid=(n // tm,),
        in_specs=[
            pl.BlockSpec((tm, d), lambda i: (i, 0)),
            pl.BlockSpec((1, 1, N_MOD * d), lambda i: (i // per_batch, 0, 0)),
            _layer_resident(norm_w.shape, layer),
            _layer_resident(w_all.shape, layer),
        ],
        out_specs=[
            pl.BlockSpec((N_PCHUNKS, tm, PCHUNK), lambda i: (0, i, 0)),
            pl.BlockSpec((tm, LR_PAD), lambda i: (i, 0)),
        ],
        out_shape=[
            jax.ShapeDtypeStruct((N_PCHUNKS, n, PCHUNK), BF16),
            jax.ShapeDtypeStruct((n, LR_PAD), BF16),
        ],
        scratch_shapes=[pltpu.VMEM((tm, d), BF16)],
        compiler_params=_cparams(("arbitrary",)),
        name="in_proj",
    )(x2, mod_l, norm_w, w_all)


NA_ROWS_PER_STEP = 8
NA_ROW_UNROLL = 4
NA_BAND = WIN_R * GRID_W
NA_VARIANTS = 8


def _na_kernel(q_ref, k_ref, v_ref, bias_ref, o_ref, *, rows_total):
    step = pl.program_id(1)
    lane = lax.broadcasted_iota(jnp.int32, (GRID_W, HEAD_LANES), 1)
    low = lane < NA_HEAD_DIM

    def row_group(gi, carry):
        items = []
        for u in range(NA_ROW_UNROLL):
            rr = gi * NA_ROW_UNROLL + u
            r = step * NA_ROWS_PER_STEP + rr
            row_start = jnp.clip(r - WIN_R // 2, 0, rows_total - WIN_R)
            variant = jnp.where(r < WIN_R // 2, r,
                                jnp.where(r > rows_total - WIN_R // 2, r - (rows_total - WIN_R), WIN_R // 2))
            k0 = pl.multiple_of(row_start * GRID_W, GRID_W)
            q0 = pl.multiple_of(rr * GRID_W, GRID_W)
            for hp in range(NA_HEADS // 2):
                items.append((q0, k0, variant, hp, slice(hp * HEAD_LANES, (hp + 1) * HEAD_LANES)))
        scores = []
        for q0, k0, variant, hp, hs in items:
            qp = q_ref[0, pl.ds(q0, GRID_W), hs]
            zero = jnp.zeros_like(qp)
            q2 = jnp.concatenate([jnp.where(low, qp, zero), jnp.where(low, zero, qp)], axis=0)
            scores.append(_dot_nt(q2, k_ref[0, pl.ds(k0, NA_BAND), hs]) + bias_ref[variant, hp])
        probs = []
        for s in scores:
            e = jnp.exp(s - jnp.max(s, axis=-1, keepdims=True))
            probs.append((e.astype(BF16), jnp.sum(e, axis=-1, keepdims=True)))
        for (q0, k0, variant, hp, hs), (e16, denom) in zip(items, probs):
            o2 = _dot(e16, v_ref[0, pl.ds(k0, NA_BAND), hs]) / denom
            o_ref[pl.ds(q0, GRID_W), hs] = jnp.where(low, o2[:GRID_W], o2[GRID_W:]).astype(BF16)
        return carry

    lax.fori_loop(0, NA_ROWS_PER_STEP // NA_ROW_UNROLL, row_group, 0)


def _na_bias_table(rpb, rows):
    wr = min(WIN_R, rows)
    col = np.arange(GRID_W)
    col_start = np.clip(col - WIN_C // 2, 0, GRID_W - WIN_C)
    col_mask = (col[None, :] >= col_start[:, None]) & (col[None, :] < col_start[:, None] + WIN_C)
    col_off = np.clip(col[None, :] - col[:, None], -(WIN_C - 1), WIN_C - 1) + (WIN_C - 1)
    n_ro, n_co = 2 * WIN_R - 1, 2 * WIN_C - 1
    onehot = (col_off.reshape(-1)[None, :] == np.arange(n_co)[:, None]).astype(np.float32)
    by_col = jnp.dot(rpb.astype(F32).reshape(NA_HEADS * n_ro, n_co), onehot, precision=lax.Precision.HIGHEST)
    by_col = by_col.reshape(NA_HEADS, n_ro, GRID_W, GRID_W)
    by_col = jnp.where(col_mask[None, None], by_col, NEG_INF)
    rep_rows = list(range(WIN_R // 2)) + [WIN_R // 2] + [rows - WIN_R + v for v in range(WIN_R // 2 + 1, WIN_R)]
    tabs = []
    for r in rep_rows:
        row_start = int(np.clip(r - WIN_R // 2, 0, rows - wr))
        ro0 = row_start - r + (WIN_R - 1)
        b = by_col[:, ro0:ro0 + wr].transpose(0, 2, 1, 3)
        tabs.append(b.reshape(NA_HEADS // 2, 2 * GRID_W, wr * GRID_W))
    return jnp.stack(tabs)


def _neighborhood_attention(p3, bias_tab, bsz, t):
    rows = t // GRID_W
    assert rows >= 2 * WIN_R and rows % NA_ROWS_PER_STEP == 0
    n = bsz * t
    tq = NA_ROWS_PER_STEP * GRID_W
    steps = t // tq
    return pl.pallas_call(
        functools.partial(_na_kernel, rows_total=rows),
        grid=(bsz, steps),
        in_specs=[
            pl.BlockSpec((1, tq, PCHUNK), lambda b, i: (CH_NA_Q, b * steps + i, 0)),
            pl.BlockSpec((1, t, PCHUNK), lambda b, i: (CH_NA_K, b, 0)),
            pl.BlockSpec((1, t, PCHUNK), lambda b, i: (CH_NA_V, b, 0)),
            _resident((NA_VARIANTS, NA_HEADS // 2, 2 * GRID_W, NA_BAND)),
        ],
        out_specs=pl.BlockSpec((tq, NA_WIDTH), lambda b, i: (b * steps + i, 0)),
        out_shape=jax.ShapeDtypeStruct((n, NA_WIDTH), BF16),
        compiler_params=_cparams(("arbitrary", "arbitrary")),
        name="neighborhood_attn",
    )(p3, p3, p3, bias_tab)


def _rec_kernel(*refs, mode, reverse, final, chunk, block):
    it = iter(refs)
    q_ref = next(it)
    if mode == "gla":
        v_ref, lr_ref, uph_ref, upl_ref, gbias_ref = (next(it) for _ in range(5))
    else:
        f_ref, v_ref, lb_ref = (next(it) for _ in range(3))
    tri_ref = next(it)
    if final:
        prev_ref, gate_ref, nw_ref = (next(it) for _ in range(3))
    o_ref = next(it)
    st_ref, st_old, kbuf, wbuf, qbuf = (next(it) for _ in range(5))

    c, s = chunk, block
    nb = c // s
    dk = GLA_DK if mode == "gla" else HEAD_LANES
    width = N_REC_HEADS * dk
    n_chunks = REC_TOKENS_PER_STEP // c

    def key_lanes(h):
        if dk == HEAD_LANES:
            return slice(h * HEAD_LANES, (h + 1) * HEAD_LANES), None
        t = (h * dk) // HEAD_LANES
        lane = lax.broadcasted_iota(jnp.int32, (1, HEAD_LANES), 1)
        mine = (lane // dk) == (h - t * (HEAD_LANES // dk))
        return slice(t * HEAD_LANES, (t + 1) * HEAD_LANES), mine

    def own(x, mine):
        return x if mine is None else jnp.where(mine, x, jnp.zeros_like(x))

    @pl.when(pl.program_id(1) == 0)
    def _():
        st_ref[...] = jnp.zeros_like(st_ref)

    ii = lax.broadcasted_iota(jnp.int32, (c, c), 0)
    jj = lax.broadcasted_iota(jnp.int32, (c, c), 1)
    bi, bj = ii // s, jj // s
    gap = (bj - bi) if reverse else (bi - bj)
    seen = (jj >= ii) if reverse else (jj <= ii)
    cls = jnp.where(gap == 0, jnp.where(seen, 0, -1), jnp.where(gap > 0, gap, -1))
    row_id = lax.broadcasted_iota(jnp.int32, (c, 1), 0)
    col_id = lax.broadcasted_iota(jnp.int32, (1, c), 1)

    def prev_block(i, steps=1):
        return i + steps if reverse else i - steps

    def per_block(vals):
        return jnp.concatenate([jnp.broadcast_to(x, (s, width)) for x in vals], axis=0)

    def chunk_inputs(rows):
        if mode == "gla":
            q = q_ref[0, rows, :width].astype(F32)
            k = q_ref[0, rows, width:].astype(F32)
            lr = lr_ref[rows, :]
            z = _dot(lr, uph_ref[...]) + _dot(lr, upl_ref[...]) + gbias_ref[...]
            g = (jnp.minimum(z, 0.0) - jnp.log(1.0 + jnp.exp(-jnp.abs(z)))) * (1.0 / GLA_TAU)
        else:
            z = f_ref[0, rows, :].astype(F32)
            sig, sig_neg = _sigmoid_pair(z)
            lb = lb_ref[...]
            g = jnp.log(jnp.maximum(lb + (1.0 - lb) * sig, F_FLOOR))
            k = (1.0 - lb) * sig_neg
            q = _silu(q_ref[0, rows, :].astype(F32))
        return q, k, g

    def chunk_factors(q, k, w, exact):
        w3 = w.reshape(nb, s, width)
        last = 0 if reverse else s - 1
        wb = [w3[i, last:last + 1, :] for i in range(nb)]
        order = list(range(nb - 1, -1, -1)) if reverse else list(range(nb))
        beta = [None] * nb
        run = jnp.zeros((1, width), F32)
        for i in order:
            beta[i] = run
            run = run + wb[i]
        total = run
        ewb = [jnp.exp(x) for x in wb]
        qd = q * jnp.exp(w)
        ks = k * jnp.exp(per_block(wb) - w)
        kd16 = None if exact else (k * jnp.exp(-w)).astype(BF16)
        one = jnp.ones((1, width), F32)
        lhs, fac = [qd.astype(BF16)], [one] * nb
        for dgap in range(2, nb):
            fac = [fac[i] * (ewb[prev_block(i, dgap - 1)] if 0 <= prev_block(i, dgap - 1) < nb else one)
                   for i in range(nb)]
            lhs.append((qd * per_block(fac)).astype(BF16))
        ks16 = ks.astype(BF16)
        if nb == 1:
            return lhs, kd16, ks16, lhs[0], ks16, jnp.exp(total)
        qe16 = (qd * jnp.exp(per_block(beta))).astype(BF16)
        ke16 = (ks * jnp.exp(per_block([total - beta[i] - wb[i] for i in range(nb)]))).astype(BF16)
        return lhs, kd16, ks16, qe16, ke16, jnp.exp(total)

    def head_scores(h, lhs, kd16, ks16, sd):
        ks_, mine = key_lanes(h)
        if sd is None:
            sd = _dot_nt(own(lhs[0][:, ks_], mine), kd16[:, ks_])
        a = jnp.where(cls == 0, sd, 0.0)
        if nb > 1:
            so = _dot_nt(jnp.concatenate([own(x[:, ks_], mine) for x in lhs], axis=0), ks16[:, ks_])
            for dgap in range(1, nb):
                a = jnp.where(cls == dgap, so[(dgap - 1) * c:dgap * c], a)
        return a.astype(BF16)

    def emit(h, rows, o):
        hs = slice(h * HEAD_LANES, (h + 1) * HEAD_LANES)
        if final:
            tot = prev_ref[rows, hs] + o
            ms = jnp.mean(tot * tot, axis=-1, keepdims=True)
            y = tot * lax.rsqrt(ms + RMS_EPS) * nw_ref[...]
            o_ref[rows, hs] = (y * _silu(gate_ref[0, rows, hs].astype(F32))).astype(o_ref.dtype)
        else:
            o_ref[rows, hs] = o

    def exact_same_block_scores(u, h):
        hs, mine = key_lanes(h)

        def col_body(jg, acc):
            j0 = pl.multiple_of(jg * SUBLANES, SUBLANES)
            k8 = kbuf[u, pl.ds(j0, SUBLANES), hs]
            w8 = wbuf[u, pl.ds(j0, SUBLANES), hs]
            for jr in range(SUBLANES):
                j = j0 + jr
                same = (row_id // s) == (j // s)
                valid = same & ((row_id <= j) if reverse else (row_id >= j))
                e = jnp.exp(jnp.where(valid, wbuf[u, :, hs] - w8[jr:jr + 1], 0.0))
                t = jnp.sum(own(qbuf[u, :, hs], mine) * k8[jr:jr + 1] * e, axis=-1, keepdims=True)
                acc = jnp.where(valid & (col_id == j), t, acc)
            return acc

        return lax.fori_loop(0, c // SUBLANES, col_body, jnp.zeros((c, c), F32))

    def group_body(gi, carry):
        all_rows = []
        for u in range(REC_UNROLL):
            ci = gi * REC_UNROLL + u
            cc = (n_chunks - 1 - ci) if reverse else ci
            all_rows.append(pl.ds(pl.multiple_of(cc * c, c), c))
        ws, facs = [], []
        for u, rows in enumerate(all_rows):
            q, k, g = chunk_inputs(rows)
            g_hi, g_lo = _split_bf16(g)
            w2 = _dot(tri_ref[...], jnp.concatenate([g_hi, g_lo], axis=1))
            w = w2[:, :width] + w2[:, width:]
            ws.append(w)
            facs.append(chunk_factors(q, k, w, exact=False))
            qbuf[u] = q
            kbuf[u] = k
            wbuf[u] = w
        a16 = [[head_scores(h, f[0], f[1], f[2], None) for h in range(N_REC_HEADS)] for f in facs]
        for h in range(N_REC_HEADS):
            hs = slice(h * HEAD_LANES, (h + 1) * HEAD_LANES)
            ks_, mine = key_lanes(h)
            st = st_ref[h]
            for u, rows in enumerate(all_rows):
                _, _, _, qe16, ke16, decay = facs[u]
                v16 = v_ref[0, rows, hs]
                st_old[u, h] = st
                emit(h, rows, _dot(a16[u][h], v16) + _dot_nt(own(qe16[:, ks_], mine), st.astype(BF16)))
                st = st * decay[:, ks_] + _dot_tn(v16, own(ke16[:, ks_], mine))
            st_ref[h] = st

        w_min = functools.reduce(jnp.minimum, ws)

        @pl.when(jnp.min(w_min) < -DECAY_SPAN_LIMIT)
        def _():
            for u, rows in enumerate(all_rows):
                lhs, _, ks16, qe16, _, _ = chunk_factors(qbuf[u], kbuf[u], wbuf[u], exact=True)
                for h in range(N_REC_HEADS):
                    hs = slice(h * HEAD_LANES, (h + 1) * HEAD_LANES)
                    ks_, mine = key_lanes(h)
                    a = head_scores(h, lhs, None, ks16, exact_same_block_scores(u, h))
                    emit(h, rows, _dot(a, v_ref[0, rows, hs])
                         + _dot_nt(own(qe16[:, ks_], mine), st_old[u, h].astype(BF16)))

        return carry

    lax.fori_loop(0, n_chunks // REC_UNROLL, group_body, 0)


def _scan_mask(chunk, block, reverse):
    i = np.arange(chunk)
    same = (i[:, None] // block) == (i[None, :] // block)
    seen = (i[None, :] >= i[:, None]) if reverse else (i[None, :] <= i[:, None])
    return jnp.asarray((same & seen).astype(np.float32), BF16)


def _recurrent_pass(mode, reverse, final, chunk, block, p3, bsz, t, extra, prev=None, norm_w=None):
    n = bsz * t
    tb = REC_TOKENS_PER_STEP
    nblk = t // tb
    width = N_REC_HEADS * HEAD_LANES
    key_width = N_REC_HEADS * (GLA_DK if mode == "gla" else HEAD_LANES)

    def tok(b, i):
        return b * nblk + ((nblk - 1 - i) if reverse else i)

    def chunk_spec(ch):
        return pl.BlockSpec((1, tb, PCHUNK), lambda b, i: (ch, tok(b, i), 0))

    tok_spec = lambda w: pl.BlockSpec((tb, w), lambda b, i: (tok(b, i), 0))
    operands, specs = [], []
    if mode == "gla":
        lr, up_hi, up_lo, gbias = extra
        operands += [p3, p3, lr, up_hi, up_lo, gbias]
        specs += [chunk_spec(CH_GLA_QK), chunk_spec(CH_GLA_V), tok_spec(LR_PAD),
                  _resident(up_hi.shape), _resident(up_lo.shape), _resident(gbias.shape)]
        gate_ch = CH_GLA_G
    else:
        (lb,) = extra
        operands += [p3, p3, p3, lb]
        specs += [chunk_spec(CH_HG_Q), chunk_spec(CH_HG_FB if reverse else CH_HG_FF), chunk_spec(CH_HG_I),
                  _resident(lb.shape)]
        gate_ch = CH_HG_G
    tri = _scan_mask(chunk, block, reverse)
    operands.append(tri)
    specs.append(_resident(tri.shape))
    if final:
        operands += [prev, p3, norm_w]
        specs += [tok_spec(width), chunk_spec(gate_ch), _resident(norm_w.shape)]
    out_dtype = BF16 if final else F32
    return pl.pallas_call(
        functools.partial(_rec_kernel, mode=mode, reverse=reverse, final=final, chunk=chunk, block=block),
        grid=(bsz, nblk),
        in_specs=specs,
        out_specs=tok_spec(width),
        out_shape=jax.ShapeDtypeStruct((n, width), out_dtype),
        scratch_shapes=[
            pltpu.VMEM((N_REC_HEADS, HEAD_LANES, HEAD_LANES), F32),
            pltpu.VMEM((REC_UNROLL, N_REC_HEADS, HEAD_LANES, HEAD_LANES), F32),
            pltpu.VMEM((REC_UNROLL, chunk, key_width), F32),
            pltpu.VMEM((REC_UNROLL, chunk, key_width), F32),
            pltpu.VMEM((REC_UNROLL, chunk, key_width), F32),
        ],
        compiler_params=_cparams(("arbitrary", "arbitrary")),
        name=f"{mode}_{'bwd' if reverse else 'fwd'}",
    )(*operands)


GLA_CHUNK, GLA_BLOCK = 128, 128
HGRN_CHUNK, HGRN_BLOCK = 64, 16


def _gla_branch(p3, lr, up_pads, gbias_pads, norm_w, bsz, t):
    fwd = _recurrent_pass("gla", False, False, GLA_CHUNK, GLA_BLOCK, p3, bsz, t,
                          (lr, up_pads[0][0], up_pads[0][1], gbias_pads[0]))
    return _recurrent_pass("gla", True, True, GLA_CHUNK, GLA_BLOCK, p3, bsz, t,
                           (lr, up_pads[1][0], up_pads[1][1], gbias_pads[1]), prev=fwd, norm_w=norm_w)


def _hgrn_branch(p3, lbs, norm_w, bsz, t):
    fwd = _recurrent_pass("hgrn", False, False, HGRN_CHUNK, HGRN_BLOCK, p3, bsz, t, (lbs[0],))
    return _recurrent_pass("hgrn", True, True, HGRN_CHUNK, HGRN_BLOCK, p3, bsz, t, (lbs[1],),
                           prev=fwd, norm_w=norm_w)


MLP_ROWS = 512
MLP_GROUP_ROWS = 256


def _mlp_kernel(x_ref, ona_ref, ogla_ref, ohg_ref, gates_ref, mod_ref, wna_ref, wgla_ref, whg_ref, wout_ref,
                n2_ref, wg_ref, wu_ref, wd_ref, fn_ref, o_ref, *, last):
    d = D_MODEL
    gate1 = mod_ref[0, :, 2 * d:3 * d]
    shift2 = mod_ref[0, :, 3 * d:4 * d]
    scale2 = mod_ref[0, :, 4 * d:5 * d]
    gate2 = mod_ref[0, :, 5 * d:6 * d]
    groups = [slice(i * MLP_GROUP_ROWS, (i + 1) * MLP_GROUP_ROWS) for i in range(MLP_ROWS // MLP_GROUP_ROWS)]

    def gate(rs, i):
        g = jnp.concatenate([gates_ref[2 * i, rs, :].astype(F32), gates_ref[2 * i + 1, rs, :].astype(F32)], axis=1)
        return _sigmoid_pair(g)[0]

    merged = [(gate(rs, 0) * _dot(ona_ref[rs, :], wna_ref[0])
               + gate(rs, 1) * _dot(ogla_ref[rs, :], wgla_ref[0])
               + gate(rs, 2) * _dot(ohg_ref[rs, :], whg_ref[0])).astype(BF16) for rs in groups]
    x1 = [x_ref[rs, :] + gate1 * _dot(m, wout_ref[0]) for rs, m in zip(groups, merged)]
    h = [_rmsnorm_mod(x, n2_ref[0], scale2, shift2).astype(BF16) for x in x1]
    act = [(_silu(_dot(hh, wg_ref[0])) * _dot(hh, wu_ref[0])).astype(BF16) for hh in h]
    for rs, x, a in zip(groups, x1, act):
        x2 = x + gate2 * _dot(a, wd_ref[0])
        if last:
            ms = jnp.mean(x2 * x2, axis=-1, keepdims=True)
            x2 = x2 * lax.rsqrt(ms + RMS_EPS) * fn_ref[...]
        o_ref[rs, :] = x2


def _merge_mlp(x2, o_na, o_gla, o_hg, p3, mod_l, layer_wts, final_norm_w, layer, tokens_per_batch, last):
    n, d = x2.shape
    tm = MLP_ROWS
    per_batch = tokens_per_batch // tm
    row = lambda w: pl.BlockSpec((tm, w), lambda i: (i, 0))
    return pl.pallas_call(
        functools.partial(_mlp_kernel, last=last),
        grid=(n // tm,),
        in_specs=[
            row(d), row(NA_WIDTH), row(NA_WIDTH), row(NA_WIDTH),
            pl.BlockSpec((N_GATE_CHUNKS, tm, PCHUNK), lambda i: (0, i, 0)),
            pl.BlockSpec((1, 1, N_MOD * d), lambda i: (i // per_batch, 0, 0)),
        ] + [_layer_resident(w.shape, layer) for w in layer_wts] + [_resident(final_norm_w.shape)],
        out_specs=row(d),
        out_shape=jax.ShapeDtypeStruct((n, d), F32),
        compiler_params=_cparams(("arbitrary",)),
        name="merge_mlp",
    )(x2, o_na, o_gla, o_hg, p3, mod_l, *layer_wts, final_norm_w)


def _layout_w_in(w):
    sizes = (NA_WIDTH, NA_WIDTH, NA_WIDTH, 256, 256, 512, 512, 2 * GLA_RANK, 512, 1024, 512, 512, 3 * D_MODEL)
    (na_q, na_k, na_v, gq, gk, gv, gg, glr, hq, hf, hi, hg, gates) = jnp.split(w, list(np.cumsum(sizes)[:-1]), axis=2)
    na_q = na_q * (NA_HEAD_DIM ** -0.5)
    gq = gq * (GLA_DK ** -0.5)
    lr_pad = jnp.zeros(w.shape[:2] + (LR_PAD - 2 * GLA_RANK,), w.dtype)
    cols = jnp.concatenate([gates, na_q, na_k, na_v, gq, gk, gv, gg, hq, hf, hi, hg, glr, lr_pad], axis=2)
    assert cols.shape[2] == INPROJ_COLS
    return cols.astype(BF16)


def kernel(x, c, w_ada, b_ada, norm1_w, w_in, na_rpb, gla_lr_up, gla_lr_bias, gla_norm_w, hgrn_lb_logits, hgrn_norm_w, w_proj_na, w_proj_gla, w_proj_hgrn, w_out, norm2_w, w_ffn_gate, w_ffn_up, w_ffn_down, final_norm_w):
    bsz, t, d = x.shape
    depth = w_in.shape[0]
    assert d == D_MODEL and t % REC_TOKENS_PER_STEP == 0 and t % INPROJ_ROWS == 0
    n = bsz * t

    mods = _modulation(c, w_ada, b_ada).reshape(depth, bsz, 1, N_MOD * d)
    lb_p = jax.nn.softmax(hgrn_lb_logits.astype(F32), axis=0)
    lb_all = jnp.clip(jnp.cumsum(lb_p, axis=0) - lb_p[0], 0.0, 1.0)

    w_all = _layout_w_in(w_in)
    norm1 = norm1_w.reshape(depth, 1, d)
    mlp_wts = [w_proj_na.astype(BF16), w_proj_gla.astype(BF16), w_proj_hgrn.astype(BF16), w_out.astype(BF16),
               norm2_w.reshape(depth, 1, d), w_ffn_gate.astype(BF16), w_ffn_up.astype(BF16),
               w_ffn_down.astype(BF16)]
    up = jnp.zeros((depth, 2, LR_PAD, GLA_HEADS * GLA_DK), F32)
    for s in range(2):
        up = up.at[:, s, s * GLA_RANK:(s + 1) * GLA_RANK].set(gla_lr_up[:, s])
    up_hi = up.astype(BF16)
    up_lo = (up - up_hi.astype(F32)).astype(BF16)

    xf = x.reshape(n, d)
    for l in range(depth):
        p3, lr = _in_projection(xf, mods[l], norm1, w_all, l, t)

        o_na = _neighborhood_attention(p3, _na_bias_table(na_rpb[l], t // GRID_W), bsz, t)

        up_pads = [(up_hi[l, s], up_lo[l, s]) for s in range(2)]
        gbias = [gla_lr_bias[l, s].reshape(1, -1) for s in range(2)]
        o_gla = _gla_branch(p3, lr, up_pads, gbias, gla_norm_w[l].reshape(1, -1), bsz, t)

        lbs = [lb_all[l, s * HGRN_KEY_WIDTH:(s + 1) * HGRN_KEY_WIDTH].reshape(1, -1) for s in range(2)]
        o_hg = _hgrn_branch(p3, lbs, hgrn_norm_w[l].reshape(1, -1), bsz, t)

        xf = _merge_mlp(xf, o_na, o_gla, o_hg, p3, mods[l], mlp_wts, final_norm_w.reshape(1, d), l, t,
                        last=(l == depth - 1))
    return xf.reshape(bsz, t, d)
```

```python
import functools

import jax
import jax.numpy as jnp
import numpy as np
from jax import lax
from jax.experimental import pallas as pl
from jax.experimental.pallas import tpu as pltpu

F32 = jnp.float32
BF16 = jnp.bfloat16

D_MODEL = 1024
GRID_W = 64
WIN_R = 8
WIN_C = 16
NA_HEADS = 8
NA_HEAD_DIM = 64
NA_WIDTH = 512
GLA_HEADS = 4
GLA_DK = 64
GLA_RANK = 16
GLA_TAU = 16.0
HGRN_KEY_WIDTH = 512
N_MOD = 6
FFN_HIDDEN = 2816
RMS_EPS = 1e-6
NEG_INF = -1e30
F_FLOOR = 1e-30

PCHUNK = 512
N_GATE_CHUNKS = 6
N_HEAD_CHUNKS = 6
N_TAIL_CHUNKS = 11
CH_NA_Q, CH_NA_K, CH_NA_V, CH_GLA_QK, CH_GLA_V, CH_GLA_G = range(N_HEAD_CHUNKS)
CH_GATES = N_HEAD_CHUNKS
CH_HG_Q, CH_HG_FF, CH_HG_FB, CH_HG_I, CH_HG_G = range(CH_GATES + N_GATE_CHUNKS, CH_GATES + N_GATE_CHUNKS + 5)
N_PCHUNKS = N_HEAD_CHUNKS + N_TAIL_CHUNKS
LR_PAD = 128
HEAD_LANES = 128
SUBLANES = 8
N_REC_HEADS = 4

VMEM_LIMIT_BYTES = 56 * 1024 * 1024

REC_TOKENS_PER_STEP = 512
REC_UNROLL = 4
DECAY_SPAN_LIMIT = 60.0


def _cparams(sem):
    return pltpu.CompilerParams(dimension_semantics=sem, vmem_limit_bytes=VMEM_LIMIT_BYTES)


def _resident(shape):
    nd = len(shape)
    return pl.BlockSpec(shape, lambda *_: (0,) * nd, pipeline_mode=pl.Buffered(1))


def _layer_resident(shape, layer):
    nd = len(shape)
    return pl.BlockSpec((1,) + tuple(shape[1:]), lambda *_: (layer,) + (0,) * (nd - 1), pipeline_mode=pl.Buffered(1))


def _dot(a, b):
    return jnp.dot(a, b, preferred_element_type=F32)


def _dot_nt(a, b):
    return lax.dot_general(a, b, (((1,), (1,)), ((), ())), preferred_element_type=F32)


def _dot_tn(a, b):
    return lax.dot_general(a, b, (((0,), (0,)), ((), ())), preferred_element_type=F32)


def _sigmoid_pair(z):
    e = jnp.exp(-jnp.abs(z))
    r = 1.0 / (1.0 + e)
    er = e * r
    pos = z >= 0
    return jnp.where(pos, r, er), jnp.where(pos, er, r)


def _silu(z):
    return z * _sigmoid_pair(z)[0]


def _split_bf16(a):
    hi = a.astype(BF16)
    lo = (a - hi.astype(F32)).astype(BF16)
    return hi, lo


MOD_COLS = 1536


def _mod_kernel(c_ref, w_ref, b_ref, o_ref):
    c_act = _silu(c_ref[...])
    c_hi, c_lo = _split_bf16(c_act)
    w_hi, w_lo = _split_bf16(w_ref[0])
    acc = _dot(c_hi, w_hi) + _dot(c_lo, w_hi) + _dot(c_hi, w_lo)
    o_ref[0] = acc + b_ref[0]


def _modulation(c, w_ada, b_ada):
    depth, d, width = w_ada.shape
    bsz = c.shape[0]
    rows = 8
    c_pad = jnp.zeros((rows, d), F32).at[:bsz].set(c)
    out = pl.pallas_call(
        _mod_kernel,
        grid=(depth, width // MOD_COLS),
        in_specs=[
            pl.BlockSpec((rows, d), lambda l, j: (0, 0)),
            pl.BlockSpec((1, d, MOD_COLS), lambda l, j: (l, 0, j)),
            pl.BlockSpec((1, 1, MOD_COLS), lambda l, j: (l, 0, j)),
        ],
        out_specs=pl.BlockSpec((1, rows, MOD_COLS), lambda l, j: (l, 0, j)),
        out_shape=jax.ShapeDtypeStruct((depth, rows, width), F32),
        compiler_params=_cparams(("arbitrary", "arbitrary")),
        name="adaln_mod",
    )(c_pad, w_ada, b_ada.reshape(depth, 1, width))
    return out[:, :bsz]


INPROJ_ROWS = 512
INPROJ_UNROLL = 3


def _rmsnorm_mod(x, w, scale, shift):
    ms = jnp.mean(x * x, axis=-1, keepdims=True)
    y = x * lax.rsqrt(ms + RMS_EPS) * w
    return y * (1.0 + scale) + shift


def _inproj_kernel(x_ref, mod_ref, nw_ref, qs_ref, wh_ref, wt_ref, wl_ref, p_ref, lr_ref, h_ref):
    d = D_MODEL
    shift = mod_ref[0, :, 0:d]
    scale = mod_ref[0, :, d:2 * d]
    h_ref[...] = _rmsnorm_mod(x_ref[...], nw_ref[0], scale, shift).astype(BF16)

    def chunk_cols(j):
        return pl.ds(pl.multiple_of(j * PCHUNK, PCHUNK), PCHUNK)

    def head_body(j, carry):
        p_ref[j] = (_dot(h_ref[...], wh_ref[0, :, chunk_cols(j)]) * qs_ref[j]).astype(BF16)
        return carry

    def gate_body(j, carry):
        p_ref[CH_GATES + j] = _dot(h_ref[...], wt_ref[0, :, chunk_cols(N_TAIL_CHUNKS - N_GATE_CHUNKS + j)]).astype(BF16)
        return carry

    lax.fori_loop(0, N_HEAD_CHUNKS, head_body, 0, unroll=INPROJ_UNROLL)
    lax.fori_loop(0, N_GATE_CHUNKS, gate_body, 0, unroll=INPROJ_UNROLL)
    for j in range(N_TAIL_CHUNKS - N_GATE_CHUNKS):
        p_ref[CH_HG_Q + j] = _dot(h_ref[...], wt_ref[0, :, j * PCHUNK:(j + 1) * PCHUNK]).astype(BF16)
    lr_ref[...] = _dot(h_ref[...], wl_ref[0]).astype(BF16)


def _in_projection(x2, mod_l, norm_w, q_scale, w_head, w_tail, w_lr, layer, tokens_per_batch):
    n, d = x2.shape
    tm = INPROJ_ROWS
    per_batch = tokens_per_batch // tm
    return pl.pallas_call(
        _inproj_kernel,
        grid=(n // tm,),
        in_specs=[
            pl.BlockSpec((tm, d), lambda i: (i, 0)),
            pl.BlockSpec((1, 1, N_MOD * d), lambda i: (i // per_batch, 0, 0)),
            _layer_resident(norm_w.shape, layer),
            _resident(q_scale.shape),
            _layer_resident(w_head.shape, layer),
            _layer_resident(w_tail.shape, layer),
            _layer_resident(w_lr.shape, layer),
        ],
        out_specs=[
            pl.BlockSpec((N_PCHUNKS, tm, PCHUNK), lambda i: (0, i, 0)),
            pl.BlockSpec((tm, LR_PAD), lambda i: (i, 0)),
        ],
        out_shape=[
            jax.ShapeDtypeStruct((N_PCHUNKS, n, PCHUNK), BF16),
            jax.ShapeDtypeStruct((n, LR_PAD), BF16),
        ],
        scratch_shapes=[pltpu.VMEM((tm, d), BF16)],
        compiler_params=_cparams(("arbitrary",)),
        name="in_proj",
    )(x2, mod_l, norm_w, q_scale, w_head, w_tail, w_lr)


NA_ROWS_PER_STEP = 8
NA_ROW_UNROLL = 4
NA_BAND = WIN_R * GRID_W
NA_VARIANTS = 8


def _na_kernel(q_ref, k_ref, v_ref, bias_ref, o_ref, *, rows_total):
    step = pl.program_id(1)
    lane = lax.broadcasted_iota(jnp.int32, (GRID_W, HEAD_LANES), 1)
    low = lane < NA_HEAD_DIM

    def row_group(gi, carry):
        items = []
        for u in range(NA_ROW_UNROLL):
            rr = gi * NA_ROW_UNROLL + u
            r = step * NA_ROWS_PER_STEP + rr
            row_start = jnp.clip(r - WIN_R // 2, 0, rows_total - WIN_R)
            variant = jnp.where(r < WIN_R // 2, r,
                                jnp.where(r > rows_total - WIN_R // 2, r - (rows_total - WIN_R), WIN_R // 2))
            k0 = pl.multiple_of(row_start * GRID_W, GRID_W)
            q0 = pl.multiple_of(rr * GRID_W, GRID_W)
            for hp in range(NA_HEADS // 2):
                items.append((q0, k0, variant, hp, slice(hp * HEAD_LANES, (hp + 1) * HEAD_LANES)))
        scores = []
        for q0, k0, variant, hp, hs in items:
            qp = q_ref[0, pl.ds(q0, GRID_W), hs]
            zero = jnp.zeros_like(qp)
            q2 = jnp.concatenate([jnp.where(low, qp, zero), jnp.where(low, zero, qp)], axis=0)
            scores.append(_dot_nt(q2, k_ref[0, pl.ds(k0, NA_BAND), hs]) + bias_ref[0, variant, hp])
        probs = []
        for s in scores:
            e = jnp.exp(s - jnp.max(s, axis=-1, keepdims=True))
            probs.append((e.astype(BF16), jnp.sum(e, axis=-1, keepdims=True)))
        for (q0, k0, variant, hp, hs), (e16, denom) in zip(items, probs):
            o2 = _dot(e16, v_ref[0, pl.ds(k0, NA_BAND), hs]) / denom
            o_ref[pl.ds(q0, GRID_W), hs] = jnp.where(low, o2[:GRID_W], o2[GRID_W:]).astype(BF16)
        return carry

    lax.fori_loop(0, NA_ROWS_PER_STEP // NA_ROW_UNROLL, row_group, 0)


def _na_bias_tables(rpb, rows):
    depth = rpb.shape[0]
    wr = min(WIN_R, rows)
    col = np.arange(GRID_W)
    col_start = np.clip(col - WIN_C // 2, 0, GRID_W - WIN_C)
    col_mask = (col[None, :] >= col_start[:, None]) & (col[None, :] < col_start[:, None] + WIN_C)
    col_off = np.clip(col[None, :] - col[:, None], -(WIN_C - 1), WIN_C - 1) + (WIN_C - 1)
    n_ro, n_co = 2 * WIN_R - 1, 2 * WIN_C - 1
    onehot = (col_off.reshape(-1)[None, :] == np.arange(n_co)[:, None]).astype(np.float32)
    by_col = jnp.dot(rpb.astype(F32).reshape(depth * NA_HEADS * n_ro, n_co), onehot, precision=lax.Precision.HIGHEST)
    by_col = by_col.reshape(depth, NA_HEADS, n_ro, GRID_W, GRID_W)
    by_col = jnp.where(col_mask[None, None, None], by_col, NEG_INF)
    rep_rows = list(range(WIN_R // 2)) + [WIN_R // 2] + [rows - WIN_R + v for v in range(WIN_R // 2 + 1, WIN_R)]
    tabs = []
    for r in rep_rows:
        row_start = int(np.clip(r - WIN_R // 2, 0, rows - wr))
        ro0 = row_start - r + (WIN_R - 1)
        tabs.append(by_col[:, :, ro0:ro0 + wr])
    tabs = jnp.stack(tabs, axis=1).transpose(0, 1, 2, 4, 3, 5)
    return tabs.reshape(depth, NA_VARIANTS, NA_HEADS // 2, 2 * GRID_W, wr * GRID_W)


def _neighborhood_attention(p3, bias_tabs, layer, bsz, t):
    rows = t // GRID_W
    assert rows >= 2 * WIN_R and rows % NA_ROWS_PER_STEP == 0
    n = bsz * t
    tq = NA_ROWS_PER_STEP * GRID_W
    steps = t // tq
    return pl.pallas_call(
        functools.partial(_na_kernel, rows_total=rows),
        grid=(bsz, steps),
        in_specs=[
            pl.BlockSpec((1, tq, PCHUNK), lambda b, i: (CH_NA_Q, b * steps + i, 0)),
            pl.BlockSpec((1, t, PCHUNK), lambda b, i: (CH_NA_K, b, 0)),
            pl.BlockSpec((1, t, PCHUNK), lambda b, i: (CH_NA_V, b, 0)),
            _layer_resident(bias_tabs.shape, layer),
        ],
        out_specs=pl.BlockSpec((tq, NA_WIDTH), lambda b, i: (b * steps + i, 0)),
        out_shape=jax.ShapeDtypeStruct((n, NA_WIDTH), BF16),
        compiler_params=_cparams(("arbitrary", "arbitrary")),
        name="neighborhood_attn",
    )(p3, p3, p3, bias_tabs)


def _rec_kernel(*refs, mode, reverse, final, chunk, block):
    it = iter(refs)
    q_ref = next(it)
    if mode == "gla":
        v_ref, lr_ref, uph_ref, upl_ref, gbias_ref = (next(it) for _ in range(5))
    else:
        f_ref, v_ref, lb_ref = (next(it) for _ in range(3))
    tri_ref = next(it)
    if final:
        prev_ref, gate_ref, nw_ref = (next(it) for _ in range(3))
    o_ref = next(it)
    st_ref, st_old, kbuf, wbuf, qbuf = (next(it) for _ in range(5))

    c, s = chunk, block
    nb = c // s
    dk = GLA_DK if mode == "gla" else HEAD_LANES
    width = N_REC_HEADS * dk
    n_chunks = REC_TOKENS_PER_STEP // c

    def key_lanes(h):
        if dk == HEAD_LANES:
            return slice(h * HEAD_LANES, (h + 1) * HEAD_LANES), None
        t = (h * dk) // HEAD_LANES
        lane = lax.broadcasted_iota(jnp.int32, (1, HEAD_LANES), 1)
        mine = (lane // dk) == (h - t * (HEAD_LANES // dk))
        return slice(t * HEAD_LANES, (t + 1) * HEAD_LANES), mine

    def own(x, mine):
        return x if mine is None else jnp.where(mine, x, jnp.zeros_like(x))

    @pl.when(pl.program_id(1) == 0)
    def _():
        st_ref[...] = jnp.zeros_like(st_ref)

    ii = lax.broadcasted_iota(jnp.int32, (c, c), 0)
    jj = lax.broadcasted_iota(jnp.int32, (c, c), 1)
    bi, bj = ii // s, jj // s
    gap = (bj - bi) if reverse else (bi - bj)
    seen = (jj >= ii) if reverse else (jj <= ii)
    cls = jnp.where(gap == 0, jnp.where(seen, 0, -1), jnp.where(gap > 0, gap, -1))
    row_id = lax.broadcasted_iota(jnp.int32, (c, 1), 0)
    col_id = lax.broadcasted_iota(jnp.int32, (1, c), 1)

    def prev_block(i, steps=1):
        return i + steps if reverse else i - steps

    def per_block(vals):
        return jnp.concatenate([jnp.broadcast_to(x, (s, width)) for x in vals], axis=0)

    def chunk_inputs(rows):
        if mode == "gla":
            q = q_ref[0, rows, :width].astype(F32)
            k = q_ref[0, rows, width:].astype(F32)
            lr = lr_ref[rows, :]
            z = _dot(lr, uph_ref[...]) + _dot(lr, upl_ref[...]) + gbias_ref[...]
            g = (jnp.minimum(z, 0.0) - jnp.log(1.0 + jnp.exp(-jnp.abs(z)))) * (1.0 / GLA_TAU)
        else:
            z = f_ref[0, rows, :].astype(F32)
            sig, sig_neg = _sigmoid_pair(z)
            lb = lb_ref[...]
            g = jnp.log(jnp.maximum(lb + (1.0 - lb) * sig, F_FLOOR))
            k = (1.0 - lb) * sig_neg
            q = _silu(q_ref[0, rows, :].astype(F32))
        return q, k, g

    def chunk_factors(q, k, w, exact):
        w3 = w.reshape(nb, s, width)
        last = 0 if reverse else s - 1
        wb = [w3[i, last:last + 1, :] for i in range(nb)]
        order = list(range(nb - 1, -1, -1)) if reverse else list(range(nb))
        beta = [None] * nb
        run = jnp.zeros((1, width), F32)
        for i in order:
            beta[i] = run
            run = run + wb[i]
        total = run
        ewb = [jnp.exp(x) for x in wb]
        qd = q * jnp.exp(w)
        ks = k * jnp.exp(per_block(wb) - w)
        kd16 = None if exact else (k * jnp.exp(-w)).astype(BF16)
        one = jnp.ones((1, width), F32)
        lhs, fac = [qd.astype(BF16)], [one] * nb
        for dgap in range(2, nb):
            fac = [fac[i] * (ewb[prev_block(i, dgap - 1)] if 0 <= prev_block(i, dgap - 1) < nb else one)
                   for i in range(nb)]
            lhs.append((qd * per_block(fac)).astype(BF16))
        ks16 = ks.astype(BF16)
        if nb == 1:
            return lhs, kd16, ks16, lhs[0], ks16, jnp.exp(total)
        qe16 = (qd * jnp.exp(per_block(beta))).astype(BF16)
        ke16 = (ks * jnp.exp(per_block([total - beta[i] - wb[i] for i in range(nb)]))).astype(BF16)
        return lhs, kd16, ks16, qe16, ke16, jnp.exp(total)

    def head_scores(h, lhs, kd16, ks16, sd):
        ks_, mine = key_lanes(h)
        if sd is None:
            sd = _dot_nt(own(lhs[0][:, ks_], mine), kd16[:, ks_])
        a = jnp.where(cls == 0, sd, 0.0)
        if nb > 1:
            so = _dot_nt(jnp.concatenate([own(x[:, ks_], mine) for x in lhs], axis=0), ks16[:, ks_])
            for dgap in range(1, nb):
                a = jnp.where(cls == dgap, so[(dgap - 1) * c:dgap * c], a)
        return a.astype(BF16)

    def emit(h, rows, o):
        hs = slice(h * HEAD_LANES, (h + 1) * HEAD_LANES)
        if final:
            tot = prev_ref[rows, hs] + o
            ms = jnp.mean(tot * tot, axis=-1, keepdims=True)
            y = tot * lax.rsqrt(ms + RMS_EPS) * nw_ref[...]
            o_ref[rows, hs] = (y * _silu(gate_ref[0, rows, hs].astype(F32))).astype(o_ref.dtype)
        else:
            o_ref[rows, hs] = o

    def exact_same_block_scores(u, h):
        hs, mine = key_lanes(h)

        def col_body(jg, acc):
            j0 = pl.multiple_of(jg * SUBLANES, SUBLANES)
            k8 = kbuf[u, pl.ds(j0, SUBLANES), hs]
            w8 = wbuf[u, pl.ds(j0, SUBLANES), hs]
            for jr in range(SUBLANES):
                j = j0 + jr
                same = (row_id // s) == (j // s)
                valid = same & ((row_id <= j) if reverse else (row_id >= j))
                e = jnp.exp(jnp.where(valid, wbuf[u, :, hs] - w8[jr:jr + 1], 0.0))
                t = jnp.sum(own(qbuf[u, :, hs], mine) * k8[jr:jr + 1] * e, axis=-1, keepdims=True)
                acc = jnp.where(valid & (col_id == j), t, acc)
            return acc

        return lax.fori_loop(0, c // SUBLANES, col_body, jnp.zeros((c, c), F32))

    def group_body(gi, carry):
        all_rows = []
        for u in range(REC_UNROLL):
            ci = gi * REC_UNROLL + u
            cc = (n_chunks - 1 - ci) if reverse else ci
            all_rows.append(pl.ds(pl.multiple_of(cc * c, c), c))
        ws, facs = [], []
        for u, rows in enumerate(all_rows):
            q, k, g = chunk_inputs(rows)
            g_hi, g_lo = _split_bf16(g)
            w2 = _dot(tri_ref[...], jnp.concatenate([g_hi, g_lo], axis=1))
            w = w2[:, :width] + w2[:, width:]
            ws.append(w)
            facs.append(chunk_factors(q, k, w, exact=False))
            qbuf[u] = q
            kbuf[u] = k
            wbuf[u] = w
        a16 = [[head_scores(h, f[0], f[1], f[2], None) for h in range(N_REC_HEADS)] for f in facs]
        for h in range(N_REC_HEADS):
            hs = slice(h * HEAD_LANES, (h + 1) * HEAD_LANES)
            ks_, mine = key_lanes(h)
            st = st_ref[h]
            for u, rows in enumerate(all_rows):
                _, _, _, qe16, ke16, decay = facs[u]
                v16 = v_ref[0, rows, hs]
                st_old[u, h] = st
                emit(h, rows, _dot(a16[u][h], v16) + _dot_nt(own(qe16[:, ks_], mine), st.astype(BF16)))
                st = st * decay[:, ks_] + _dot_tn(v16, own(ke16[:, ks_], mine))
            st_ref[h] = st

        w_min = functools.reduce(jnp.minimum, ws)

        @pl.when(jnp.min(w_min) < -DECAY_SPAN_LIMIT)
        def _():
            for u, rows in enumerate(all_rows):
                lhs, _, ks16, qe16, _, _ = chunk_factors(qbuf[u], kbuf[u], wbuf[u], exact=True)
                for h in range(N_REC_HEADS):
                    hs = slice(h * HEAD_LANES, (h + 1) * HEAD_LANES)
                    ks_, mine = key_lanes(h)
                    a = head_scores(h, lhs, None, ks16, exact_same_block_scores(u, h))
                    emit(h, rows, _dot(a, v_ref[0, rows, hs])
                         + _dot_nt(own(qe16[:, ks_], mine), st_old[u, h].astype(BF16)))

        return carry

    lax.fori_loop(0, n_chunks // REC_UNROLL, group_body, 0)


def _scan_mask(chunk, block, reverse):
    i = np.arange(chunk)
    same = (i[:, None] // block) == (i[None, :] // block)
    seen = (i[None, :] >= i[:, None]) if reverse else (i[None, :] <= i[:, None])
    return jnp.asarray((same & seen).astype(np.float32), BF16)


def _recurrent_pass(mode, reverse, final, chunk, block, p3, bsz, t, extra, prev=None, norm_w=None):
    n = bsz * t
    tb = REC_TOKENS_PER_STEP
    nblk = t // tb
    width = N_REC_HEADS * HEAD_LANES
    key_width = N_REC_HEADS * (GLA_DK if mode == "gla" else HEAD_LANES)

    def tok(b, i):
        return b * nblk + ((nblk - 1 - i) if reverse else i)

    def chunk_spec(ch):
        return pl.BlockSpec((1, tb, PCHUNK), lambda b, i: (ch, tok(b, i), 0))

    tok_spec = lambda w: pl.BlockSpec((tb, w), lambda b, i: (tok(b, i), 0))
    operands, specs = [], []
    if mode == "gla":
        lr, up_hi, up_lo, gbias = extra
        operands += [p3, p3, lr, up_hi, up_lo, gbias]
        specs += [chunk_spec(CH_GLA_QK), chunk_spec(CH_GLA_V), tok_spec(LR_PAD),
                  _resident(up_hi.shape), _resident(up_lo.shape), _resident(gbias.shape)]
        gate_ch = CH_GLA_G
    else:
        (lb,) = extra
        operands += [p3, p3, p3, lb]
        specs += [chunk_spec(CH_HG_Q), chunk_spec(CH_HG_FB if reverse else CH_HG_FF), chunk_spec(CH_HG_I),
                  _resident(lb.shape)]
        gate_ch = CH_HG_G
    tri = _scan_mask(chunk, block, reverse)
    operands.append(tri)
    specs.append(_resident(tri.shape))
    if final:
        operands += [prev, p3, norm_w]
        specs += [tok_spec(width), chunk_spec(gate_ch), _resident(norm_w.shape)]
    out_dtype = BF16 if final else F32
    return pl.pallas_call(
        functools.partial(_rec_kernel, mode=mode, reverse=reverse, final=final, chunk=chunk, block=block),
        grid=(bsz, nblk),
        in_specs=specs,
        out_specs=tok_spec(width),
        out_shape=jax.ShapeDtypeStruct((n, width), out_dtype),
        scratch_shapes=[
            pltpu.VMEM((N_REC_HEADS, HEAD_LANES, HEAD_LANES), F32),
            pltpu.VMEM((REC_UNROLL, N_REC_HEADS, HEAD_LANES, HEAD_LANES), F32),
            pltpu.VMEM((REC_UNROLL, chunk, key_width), F32),
            pltpu.VMEM((REC_UNROLL, chunk, key_width), F32),
            pltpu.VMEM((REC_UNROLL, chunk, key_width), F32),
        ],
        compiler_params=_cparams(("arbitrary", "arbitrary")),
        name=f"{mode}_{'bwd' if reverse else 'fwd'}",
    )(*operands)


GLA_CHUNK, GLA_BLOCK = 128, 128
HGRN_CHUNK, HGRN_BLOCK = 64, 16


def _gla_branch(p3, lr, up_pads, gbias_pads, norm_w, bsz, t):
    fwd = _recurrent_pass("gla", False, False, GLA_CHUNK, GLA_BLOCK, p3, bsz, t,
                          (lr, up_pads[0][0], up_pads[0][1], gbias_pads[0]))
    return _recurrent_pass("gla", True, True, GLA_CHUNK, GLA_BLOCK, p3, bsz, t,
                           (lr, up_pads[1][0], up_pads[1][1], gbias_pads[1]), prev=fwd, norm_w=norm_w)


def _hgrn_branch(p3, lbs, norm_w, bsz, t):
    fwd = _recurrent_pass("hgrn", False, False, HGRN_CHUNK, HGRN_BLOCK, p3, bsz, t, (lbs[0],))
    return _recurrent_pass("hgrn", True, True, HGRN_CHUNK, HGRN_BLOCK, p3, bsz, t, (lbs[1],),
                           prev=fwd, norm_w=norm_w)


MLP_ROWS = 512
MLP_GROUP_ROWS = 256


def _mlp_kernel(x_ref, ona_ref, ogla_ref, ohg_ref, gates_ref, mod_ref, wna_ref, wgla_ref, whg_ref, wout_ref,
                n2_ref, wg_ref, wu_ref, wd_ref, fn_ref, o_ref, *, last):
    d = D_MODEL
    gate1 = mod_ref[0, :, 2 * d:3 * d]
    shift2 = mod_ref[0, :, 3 * d:4 * d]
    scale2 = mod_ref[0, :, 4 * d:5 * d]
    gate2 = mod_ref[0, :, 5 * d:6 * d]
    groups = [slice(i * MLP_GROUP_ROWS, (i + 1) * MLP_GROUP_ROWS) for i in range(MLP_ROWS // MLP_GROUP_ROWS)]

    def gate(rs, i):
        g = jnp.concatenate([gates_ref[2 * i, rs, :].astype(F32), gates_ref[2 * i + 1, rs, :].astype(F32)], axis=1)
        return _sigmoid_pair(g)[0]

    merged = [(gate(rs, 0) * _dot(ona_ref[rs, :], wna_ref[0])
               + gate(rs, 1) * _dot(ogla_ref[rs, :], wgla_ref[0])
               + gate(rs, 2) * _dot(ohg_ref[rs, :], whg_ref[0])).astype(BF16) for rs in groups]
    x1 = [x_ref[rs, :] + gate1 * _dot(m, wout_ref[0]) for rs, m in zip(groups, merged)]
    h = [_rmsnorm_mod(x, n2_ref[0], scale2, shift2).astype(BF16) for x in x1]
    act = [(_silu(_dot(hh, wg_ref[0])) * _dot(hh, wu_ref[0])).astype(BF16) for hh in h]
    for rs, x, a in zip(groups, x1, act):
        x2 = x + gate2 * _dot(a, wd_ref[0])
        if last:
            ms = jnp.mean(x2 * x2, axis=-1, keepdims=True)
            x2 = x2 * lax.rsqrt(ms + RMS_EPS) * fn_ref[...]
        o_ref[rs, :] = x2


def _merge_mlp(x2, o_na, o_gla, o_hg, p3, mod_l, layer_wts, final_norm_w, layer, tokens_per_batch, last):
    n, d = x2.shape
    tm = MLP_ROWS
    per_batch = tokens_per_batch // tm
    row = lambda w: pl.BlockSpec((tm, w), lambda i: (i, 0))
    return pl.pallas_call(
        functools.partial(_mlp_kernel, last=last),
        grid=(n // tm,),
        in_specs=[
            row(d), row(NA_WIDTH), row(NA_WIDTH), row(NA_WIDTH),
            pl.BlockSpec((N_GATE_CHUNKS, tm, PCHUNK), lambda i: (CH_GATES // N_GATE_CHUNKS, i, 0)),
            pl.BlockSpec((1, 1, N_MOD * d), lambda i: (i // per_batch, 0, 0)),
        ] + [_layer_resident(w.shape, layer) for w in layer_wts] + [_resident(final_norm_w.shape)],
        out_specs=row(d),
        out_shape=jax.ShapeDtypeStruct((n, d), F32),
        compiler_params=_cparams(("arbitrary",)),
        name="merge_mlp",
    )(x2, o_na, o_gla, o_hg, p3, mod_l, *layer_wts, final_norm_w)


def _split_w_in(w):
    head = N_HEAD_CHUNKS * PCHUNK
    tail = head + 2 * GLA_RANK
    assert w.shape[2] - tail == N_TAIL_CHUNKS * PCHUNK
    w_lr = jnp.pad(w[:, :, head:tail], ((0, 0), (0, 0), (0, LR_PAD - 2 * GLA_RANK)))
    return w[:, :, :head].astype(BF16), w[:, :, tail:].astype(BF16), w_lr.astype(BF16)


def _query_scales():
    qs = np.ones((N_HEAD_CHUNKS, 1, PCHUNK), np.float32)
    qs[CH_NA_Q] = NA_HEAD_DIM ** -0.5
    qs[CH_GLA_QK, :, :GLA_HEADS * GLA_DK] = GLA_DK ** -0.5
    return jnp.asarray(qs)


def kernel(x, c, w_ada, b_ada, norm1_w, w_in, na_rpb, gla_lr_up, gla_lr_bias, gla_norm_w, hgrn_lb_logits, hgrn_norm_w, w_proj_na, w_proj_gla, w_proj_hgrn, w_out, norm2_w, w_ffn_gate, w_ffn_up, w_ffn_down, final_norm_w):
    bsz, t, d = x.shape
    depth = w_in.shape[0]
    assert d == D_MODEL and t % REC_TOKENS_PER_STEP == 0 and t % INPROJ_ROWS == 0
    n = bsz * t

    mods = _modulation(c, w_ada, b_ada).reshape(depth, bsz, 1, N_MOD * d)
    lb_p = jax.nn.softmax(hgrn_lb_logits.astype(F32), axis=0)
    lb_all = jnp.clip(jnp.cumsum(lb_p, axis=0) - lb_p[0], 0.0, 1.0)

    w_head, w_tail, w_lr = _split_w_in(w_in)
    q_scale = _query_scales()
    bias_tabs = _na_bias_tables(na_rpb, t // GRID_W)
    norm1 = norm1_w.reshape(depth, 1, d)
    mlp_wts = [w_proj_na.astype(BF16), w_proj_gla.astype(BF16), w_proj_hgrn.astype(BF16), w_out.astype(BF16),
               norm2_w.reshape(depth, 1, d), w_ffn_gate.astype(BF16), w_ffn_up.astype(BF16),
               w_ffn_down.astype(BF16)]
    up = jnp.zeros((depth, 2, LR_PAD, GLA_HEADS * GLA_DK), F32)
    for s in range(2):
        up = up.at[:, s, s * GLA_RANK:(s + 1) * GLA_RANK].set(gla_lr_up[:, s])
    up_hi = up.astype(BF16)
    up_lo = (up - up_hi.astype(F32)).astype(BF16)

    xf = x.reshape(n, d)
    for l in range(depth):
        p3, lr = _in_projection(xf, mods[l], norm1, q_scale, w_head, w_tail, w_lr, l, t)

        o_na = _neighborhood_attention(p3, bias_tabs, l, bsz, t)

        up_pads = [(up_hi[l, s], up_lo[l, s]) for s in range(2)]
        gbias = [gla_lr_bias[l, s].reshape(1, -1) for s in range(2)]
        o_gla = _gla_branch(p3, lr, up_pads, gbias, gla_norm_w[l].reshape(1, -1), bsz, t)

        lbs = [lb_all[l, s * HGRN_KEY_WIDTH:(s + 1) * HGRN_KEY_WIDTH].reshape(1, -1) for s in range(2)]
        o_hg = _hgrn_branch(p3, lbs, hgrn_norm_w[l].reshape(1, -1), bsz, t)

        xf = _merge_mlp(xf, o_na, o_gla, o_hg, p3, mods[l], mlp_wts, final_norm_w.reshape(1, d), l, t,
                        last=(l == depth - 1))
    return xf.reshape(bsz, t, d)
```

```python
import functools

import jax
import jax.numpy as jnp
import numpy as np
from jax import lax
from jax.experimental import pallas as pl
from jax.experimental.pallas import tpu as pltpu

F32 = jnp.float32
BF16 = jnp.bfloat16

D_MODEL = 1024
GRID_W = 64
WIN_R = 8
WIN_C = 16
NA_HEADS = 8
NA_HEAD_DIM = 64
NA_WIDTH = 512
GLA_HEADS = 4
GLA_DK = 64
GLA_RANK = 16
GLA_TAU = 16.0
HGRN_KEY_WIDTH = 512
N_MOD = 6
FFN_HIDDEN = 2816
RMS_EPS = 1e-6
NEG_INF = -1e30
F_FLOOR = 1e-30

PCHUNK = 512
N_GATE_CHUNKS = 6
N_HEAD_CHUNKS = 6
N_TAIL_CHUNKS = 11
CH_NA_Q, CH_NA_K, CH_NA_V, CH_GLA_QK, CH_GLA_V, CH_GLA_G = range(N_HEAD_CHUNKS)
CH_GATES = N_HEAD_CHUNKS
CH_HG_Q, CH_HG_FF, CH_HG_FB, CH_HG_I, CH_HG_G = range(CH_GATES + N_GATE_CHUNKS, CH_GATES + N_GATE_CHUNKS + 5)
N_PCHUNKS = N_HEAD_CHUNKS + N_TAIL_CHUNKS
LR_PAD = 128
HEAD_LANES = 128
SUBLANES = 8
N_REC_HEADS = 4

VMEM_LIMIT_BYTES = 56 * 1024 * 1024

REC_TOKENS_PER_STEP = 512
REC_UNROLL = 4
DECAY_SPAN_LIMIT = 60.0


def _cparams(sem):
    return pltpu.CompilerParams(dimension_semantics=sem, vmem_limit_bytes=VMEM_LIMIT_BYTES)


def _resident(shape):
    nd = len(shape)
    return pl.BlockSpec(shape, lambda *_: (0,) * nd, pipeline_mode=pl.Buffered(1))


def _layer_resident(shape, layer):
    nd = len(shape)
    return pl.BlockSpec((1,) + tuple(shape[1:]), lambda *_: (layer,) + (0,) * (nd - 1), pipeline_mode=pl.Buffered(1))


def _dot(a, b):
    return jnp.dot(a, b, preferred_element_type=F32)


def _dot_nt(a, b):
    return lax.dot_general(a, b, (((1,), (1,)), ((), ())), preferred_element_type=F32)


def _dot_tn(a, b):
    return lax.dot_general(a, b, (((0,), (0,)), ((), ())), preferred_element_type=F32)


def _sigmoid_pair(z):
    e = jnp.exp(-jnp.abs(z))
    r = 1.0 / (1.0 + e)
    er = e * r
    pos = z >= 0
    return jnp.where(pos, r, er), jnp.where(pos, er, r)


def _silu(z):
    return z * _sigmoid_pair(z)[0]


def _split_bf16(a):
    hi = a.astype(BF16)
    lo = (a - hi.astype(F32)).astype(BF16)
    return hi, lo


MOD_COLS = 1536


def _mod_kernel(c_ref, w_ref, b_ref, o_ref):
    c_act = _silu(c_ref[...])
    c_hi, c_lo = _split_bf16(c_act)
    w_hi, w_lo = _split_bf16(w_ref[0])
    acc = _dot(c_hi, w_hi) + _dot(c_lo, w_hi) + _dot(c_hi, w_lo)
    o_ref[0] = acc + b_ref[0]


def _modulation(c, w_ada, b_ada):
    depth, d, width = w_ada.shape
    bsz = c.shape[0]
    rows = 8
    c_pad = jnp.zeros((rows, d), F32).at[:bsz].set(c)
    out = pl.pallas_call(
        _mod_kernel,
        grid=(depth, width // MOD_COLS),
        in_specs=[
            pl.BlockSpec((rows, d), lambda l, j: (0, 0)),
            pl.BlockSpec((1, d, MOD_COLS), lambda l, j: (l, 0, j)),
            pl.BlockSpec((1, 1, MOD_COLS), lambda l, j: (l, 0, j)),
        ],
        out_specs=pl.BlockSpec((1, rows, MOD_COLS), lambda l, j: (l, 0, j)),
        out_shape=jax.ShapeDtypeStruct((depth, rows, width), F32),
        compiler_params=_cparams(("arbitrary", "arbitrary")),
        name="adaln_mod",
    )(c_pad, w_ada, b_ada.reshape(depth, 1, width))
    return out[:, :bsz]


INPROJ_ROWS = 512
INPROJ_UNROLL = 3


def _rmsnorm_mod(x, w, scale, shift):
    ms = jnp.mean(x * x, axis=-1, keepdims=True)
    y = x * lax.rsqrt(ms + RMS_EPS) * w
    return y * (1.0 + scale) + shift


def _inproj_kernel(x_ref, mod_ref, nw_ref, qs_ref, wh_ref, wt_ref, wl_ref, p_ref, lr_ref, h_ref):
    d = D_MODEL
    shift = mod_ref[0, :, 0:d]
    scale = mod_ref[0, :, d:2 * d]
    h_ref[...] = _rmsnorm_mod(x_ref[...], nw_ref[0], scale, shift).astype(BF16)

    def chunk_cols(j):
        return pl.ds(pl.multiple_of(j * PCHUNK, PCHUNK), PCHUNK)

    def head_body(j, carry):
        p_ref[j] = (_dot(h_ref[...], wh_ref[0, :, chunk_cols(j)]) * qs_ref[j]).astype(BF16)
        return carry

    def gate_body(j, carry):
        p_ref[CH_GATES + j] = _dot(h_ref[...], wt_ref[0, :, chunk_cols(N_TAIL_CHUNKS - N_GATE_CHUNKS + j)]).astype(BF16)
        return carry

    lax.fori_loop(0, N_HEAD_CHUNKS, head_body, 0, unroll=INPROJ_UNROLL)
    lax.fori_loop(0, N_GATE_CHUNKS, gate_body, 0, unroll=INPROJ_UNROLL)
    for j in range(N_TAIL_CHUNKS - N_GATE_CHUNKS):
        p_ref[CH_HG_Q + j] = _dot(h_ref[...], wt_ref[0, :, j * PCHUNK:(j + 1) * PCHUNK]).astype(BF16)
    lr_ref[...] = _dot(h_ref[...], wl_ref[0]).astype(BF16)


def _in_projection(x2, mod_l, norm_w, q_scale, w_head, w_tail, w_lr, layer, tokens_per_batch):
    n, d = x2.shape
    tm = INPROJ_ROWS
    per_batch = tokens_per_batch // tm
    return pl.pallas_call(
        _inproj_kernel,
        grid=(n // tm,),
        in_specs=[
            pl.BlockSpec((tm, d), lambda i: (i, 0)),
            pl.BlockSpec((1, 1, N_MOD * d), lambda i: (i // per_batch, 0, 0)),
            _layer_resident(norm_w.shape, layer),
            _resident(q_scale.shape),
            _layer_resident(w_head.shape, layer),
            _layer_resident(w_tail.shape, layer),
            _layer_resident(w_lr.shape, layer),
        ],
        out_specs=[
            pl.BlockSpec((N_PCHUNKS, tm, PCHUNK), lambda i: (0, i, 0)),
            pl.BlockSpec((tm, LR_PAD), lambda i: (i, 0)),
        ],
        out_shape=[
            jax.ShapeDtypeStruct((N_PCHUNKS, n, PCHUNK), BF16),
            jax.ShapeDtypeStruct((n, LR_PAD), BF16),
        ],
        scratch_shapes=[pltpu.VMEM((tm, d), BF16)],
        compiler_params=_cparams(("arbitrary",)),
        name="in_proj",
    )(x2, mod_l, norm_w, q_scale, w_head, w_tail, w_lr)


NA_ROWS_PER_STEP = 8
NA_ROW_UNROLL = 4
NA_BAND = WIN_R * GRID_W
NA_VARIANTS = 8


def _na_kernel(q_ref, k_ref, v_ref, bias_ref, o_ref, *, rows_total):
    step = pl.program_id(1)
    lane = lax.broadcasted_iota(jnp.int32, (GRID_W, HEAD_LANES), 1)
    low = lane < NA_HEAD_DIM

    def row_group(gi, carry):
        items = []
        for u in range(NA_ROW_UNROLL):
            rr = gi * NA_ROW_UNROLL + u
            r = step * NA_ROWS_PER_STEP + rr
            row_start = jnp.clip(r - WIN_R // 2, 0, rows_total - WIN_R)
            variant = jnp.where(r < WIN_R // 2, r,
                                jnp.where(r > rows_total - WIN_R // 2, r - (rows_total - WIN_R), WIN_R // 2))
            k0 = pl.multiple_of(row_start * GRID_W, GRID_W)
            q0 = pl.multiple_of(rr * GRID_W, GRID_W)
            for hp in range(NA_HEADS // 2):
                items.append((q0, k0, variant, hp, slice(hp * HEAD_LANES, (hp + 1) * HEAD_LANES)))
        scores = []
        for q0, k0, variant, hp, hs in items:
            qp = q_ref[0, pl.ds(q0, GRID_W), hs]
            zero = jnp.zeros_like(qp)
            q2 = jnp.concatenate([jnp.where(low, qp, zero), jnp.where(low, zero, qp)], axis=0)
            scores.append(_dot_nt(q2, k_ref[0, pl.ds(k0, NA_BAND), hs]) + bias_ref[0, variant, hp])
        probs = []
        for s in scores:
            e = jnp.exp(s - jnp.max(s, axis=-1, keepdims=True))
            probs.append((e.astype(BF16), jnp.sum(e, axis=-1, keepdims=True)))
        for (q0, k0, variant, hp, hs), (e16, denom) in zip(items, probs):
            o2 = _dot(e16, v_ref[0, pl.ds(k0, NA_BAND), hs]) / denom
            o_ref[pl.ds(q0, GRID_W), hs] = jnp.where(low, o2[:GRID_W], o2[GRID_W:]).astype(BF16)
        return carry

    lax.fori_loop(0, NA_ROWS_PER_STEP // NA_ROW_UNROLL, row_group, 0)


def _na_bias_tables(rpb, rows):
    depth = rpb.shape[0]
    wr = min(WIN_R, rows)
    col = np.arange(GRID_W)
    col_start = np.clip(col - WIN_C // 2, 0, GRID_W - WIN_C)
    col_mask = (col[None, :] >= col_start[:, None]) & (col[None, :] < col_start[:, None] + WIN_C)
    col_off = np.clip(col[None, :] - col[:, None], -(WIN_C - 1), WIN_C - 1) + (WIN_C - 1)
    n_ro, n_co = 2 * WIN_R - 1, 2 * WIN_C - 1
    onehot = (col_off.reshape(-1)[None, :] == np.arange(n_co)[:, None]).astype(np.float32)
    by_col = jnp.dot(rpb.astype(F32).reshape(depth * NA_HEADS * n_ro, n_co), onehot, precision=lax.Precision.HIGHEST)
    by_col = by_col.reshape(depth, NA_HEADS, n_ro, GRID_W, GRID_W)
    by_col = jnp.where(col_mask[None, None, None], by_col, NEG_INF)
    rep_rows = list(range(WIN_R // 2)) + [WIN_R // 2] + [rows - WIN_R + v for v in range(WIN_R // 2 + 1, WIN_R)]
    tabs = []
    for r in rep_rows:
        row_start = int(np.clip(r - WIN_R // 2, 0, rows - wr))
        ro0 = row_start - r + (WIN_R - 1)
        tabs.append(by_col[:, :, ro0:ro0 + wr])
    tabs = jnp.stack(tabs, axis=1).transpose(0, 1, 2, 4, 3, 5)
    return tabs.reshape(depth, NA_VARIANTS, NA_HEADS // 2, 2 * GRID_W, wr * GRID_W)


def _neighborhood_attention(p3, bias_tabs, layer, bsz, t):
    rows = t // GRID_W
    assert rows >= 2 * WIN_R and rows % NA_ROWS_PER_STEP == 0
    n = bsz * t
    tq = NA_ROWS_PER_STEP * GRID_W
    steps = t // tq
    return pl.pallas_call(
        functools.partial(_na_kernel, rows_total=rows),
        grid=(bsz, steps),
        in_specs=[
            pl.BlockSpec((1, tq, PCHUNK), lambda b, i: (CH_NA_Q, b * steps + i, 0)),
            pl.BlockSpec((1, t, PCHUNK), lambda b, i: (CH_NA_K, b, 0)),
            pl.BlockSpec((1, t, PCHUNK), lambda b, i: (CH_NA_V, b, 0)),
            _layer_resident(bias_tabs.shape, layer),
        ],
        out_specs=pl.BlockSpec((tq, NA_WIDTH), lambda b, i: (b * steps + i, 0)),
        out_shape=jax.ShapeDtypeStruct((n, NA_WIDTH), BF16),
        compiler_params=_cparams(("arbitrary", "arbitrary")),
        name="neighborhood_attn",
    )(p3, p3, p3, bias_tabs)


def _rec_kernel(*refs, mode, reverse, final, chunk, block):
    it = iter(refs)
    q_ref = next(it)
    if mode == "gla":
        v_ref, lr_ref, uph_ref, upl_ref, gbias_ref = (next(it) for _ in range(5))
    else:
        f_ref, v_ref, lb_ref = (next(it) for _ in range(3))
    tri_ref = next(it)
    if final:
        prev_ref, gate_ref, nw_ref = (next(it) for _ in range(3))
    o_ref = next(it)
    st_ref, st_old, kbuf, wbuf, qbuf = (next(it) for _ in range(5))

    c, s = chunk, block
    nb = c // s
    dk = GLA_DK if mode == "gla" else HEAD_LANES
    width = N_REC_HEADS * dk
    n_chunks = REC_TOKENS_PER_STEP // c

    def key_lanes(h):
        if dk == HEAD_LANES:
            return slice(h * HEAD_LANES, (h + 1) * HEAD_LANES), None
        t = (h * dk) // HEAD_LANES
        lane = lax.broadcasted_iota(jnp.int32, (1, HEAD_LANES), 1)
        mine = (lane // dk) == (h - t * (HEAD_LANES // dk))
        return slice(t * HEAD_LANES, (t + 1) * HEAD_LANES), mine

    def own(x, mine):
        return x if mine is None else jnp.where(mine, x, jnp.zeros_like(x))

    @pl.when(pl.program_id(1) == 0)
    def _():
        st_ref[...] = jnp.zeros_like(st_ref)

    ii = lax.broadcasted_iota(jnp.int32, (c, c), 0)
    jj = lax.broadcasted_iota(jnp.int32, (c, c), 1)
    bi, bj = ii // s, jj // s
    gap = (bj - bi) if reverse else (bi - bj)
    seen = (jj >= ii) if reverse else (jj <= ii)
    cls = jnp.where(gap == 0, jnp.where(seen, 0, -1), jnp.where(gap > 0, gap, -1))
    row_id = lax.broadcasted_iota(jnp.int32, (c, 1), 0)
    col_id = lax.broadcasted_iota(jnp.int32, (1, c), 1)

    def prev_block(i, steps=1):
        return i + steps if reverse else i - steps

    def per_block(vals):
        return jnp.concatenate([jnp.broadcast_to(x, (s, width)) for x in vals], axis=0)

    def chunk_inputs(rows):
        if mode == "gla":
            q = q_ref[0, rows, :width].astype(F32)
            k = q_ref[0, rows, width:].astype(F32)
            lr = lr_ref[rows, :]
            z = _dot(lr, uph_ref[...]) + _dot(lr, upl_ref[...]) + gbias_ref[...]
            g = (jnp.minimum(z, 0.0) - jnp.log(1.0 + jnp.exp(-jnp.abs(z)))) * (1.0 / GLA_TAU)
        else:
            z = f_ref[0, rows, :].astype(F32)
            sig, sig_neg = _sigmoid_pair(z)
            lb = lb_ref[...]
            g = jnp.log(jnp.maximum(lb + (1.0 - lb) * sig, F_FLOOR))
            k = (1.0 - lb) * sig_neg
            q = _silu(q_ref[0, rows, :].astype(F32))
        return q, k, g

    def chunk_factors(q, k, w, exact):
        w3 = w.reshape(nb, s, width)
        last = 0 if reverse else s - 1
        wb = [w3[i, last:last + 1, :] for i in range(nb)]
        order = list(range(nb - 1, -1, -1)) if reverse else list(range(nb))
        beta = [None] * nb
        run = jnp.zeros((1, width), F32)
        for i in order:
            beta[i] = run
            run = run + wb[i]
        total = run
        ewb = [jnp.exp(x) for x in wb]
        qd = q * jnp.exp(w)
        ks = k * jnp.exp(per_block(wb) - w)
        kd16 = None if exact else (k * jnp.exp(-w)).astype(BF16)
        one = jnp.ones((1, width), F32)
        lhs, fac = [qd.astype(BF16)], [one] * nb
        for dgap in range(2, nb):
            fac = [fac[i] * (ewb[prev_block(i, dgap - 1)] if 0 <= prev_block(i, dgap - 1) < nb else one)
                   for i in range(nb)]
            lhs.append((qd * per_block(fac)).astype(BF16))
        ks16 = ks.astype(BF16)
        if nb == 1:
            return lhs, kd16, ks16, lhs[0], ks16, jnp.exp(total)
        qe16 = (qd * jnp.exp(per_block(beta))).astype(BF16)
        ke16 = (ks * jnp.exp(per_block([total - beta[i] - wb[i] for i in range(nb)]))).astype(BF16)
        return lhs, kd16, ks16, qe16, ke16, jnp.exp(total)

    def head_scores(h, lhs, kd16, ks16, sd):
        ks_, mine = key_lanes(h)
        if sd is None:
            sd = _dot_nt(own(lhs[0][:, ks_], mine), kd16[:, ks_])
        a = jnp.where(cls == 0, sd, 0.0)
        if nb > 1:
            def far_rows(dgap):
                return slice(0, c - dgap * s) if reverse else slice(dgap * s, c)

            parts = [own(lhs[max(dgap - 1, 0)][:, ks_], mine)[far_rows(dgap)] for dgap in range(1, nb)]
            so = _dot_nt(jnp.concatenate(parts, axis=0), ks16[:, ks_])
            start = 0
            for dgap in range(1, nb):
                n_far = c - dgap * s
                blank = jnp.zeros((dgap * s, c), F32)
                part = so[start:start + n_far]
                start += n_far
                full = jnp.concatenate([part, blank] if reverse else [blank, part], axis=0)
                a = jnp.where(cls == dgap, full, a)
        return a.astype(BF16)

    def chunk_output(h, a16, qe16, v16, st):
        ks_, mine = key_lanes(h)
        lhs = jnp.concatenate([own(qe16[:, ks_], mine), a16], axis=1)
        return _dot(lhs, jnp.concatenate([st.astype(BF16), v16], axis=0))

    def emit(h, rows, o):
        hs = slice(h * HEAD_LANES, (h + 1) * HEAD_LANES)
        if final:
            tot = prev_ref[rows, hs] + o
            ms = jnp.mean(tot * tot, axis=-1, keepdims=True)
            y = tot * lax.rsqrt(ms + RMS_EPS) * nw_ref[...]
            o_ref[rows, hs] = (y * _silu(gate_ref[0, rows, hs].astype(F32))).astype(o_ref.dtype)
        else:
            o_ref[rows, hs] = o

    def exact_same_block_scores(u, h):
        hs, mine = key_lanes(h)

        def col_body(jg, acc):
            j0 = pl.multiple_of(jg * SUBLANES, SUBLANES)
            k8 = kbuf[u, pl.ds(j0, SUBLANES), hs]
            w8 = wbuf[u, pl.ds(j0, SUBLANES), hs]
            for jr in range(SUBLANES):
                j = j0 + jr
                same = (row_id // s) == (j // s)
                valid = same & ((row_id <= j) if reverse else (row_id >= j))
                e = jnp.exp(jnp.where(valid, wbuf[u, :, hs] - w8[jr:jr + 1], 0.0))
                t = jnp.sum(own(qbuf[u, :, hs], mine) * k8[jr:jr + 1] * e, axis=-1, keepdims=True)
                acc = jnp.where(valid & (col_id == j), t, acc)
            return acc

        return lax.fori_loop(0, c // SUBLANES, col_body, jnp.zeros((c, c), F32))

    def group_body(gi, carry):
        all_rows = []
        for u in range(REC_UNROLL):
            ci = gi * REC_UNROLL + u
            cc = (n_chunks - 1 - ci) if reverse else ci
            all_rows.append(pl.ds(pl.multiple_of(cc * c, c), c))
        ws, facs = [], []
        for u, rows in enumerate(all_rows):
            q, k, g = chunk_inputs(rows)
            g_hi, g_lo = _split_bf16(g)
            w2 = _dot(tri_ref[...], jnp.concatenate([g_hi, g_lo], axis=1))
            w = w2[:, :width] + w2[:, width:]
            ws.append(w)
            facs.append(chunk_factors(q, k, w, exact=False))
            qbuf[u] = q
            kbuf[u] = k
            wbuf[u] = w
        a16 = [[head_scores(h, f[0], f[1], f[2], None) for h in range(N_REC_HEADS)] for f in facs]
        vs, incs = [], []
        for u, rows in enumerate(all_rows):
            ke16 = facs[u][4]
            vs.append([v_ref[0, rows, h * HEAD_LANES:(h + 1) * HEAD_LANES] for h in range(N_REC_HEADS)])
            incs.append([_dot_tn(own(ke16[:, key_lanes(h)[0]], key_lanes(h)[1]), vs[u][h])
                         for h in range(N_REC_HEADS)])
        states = []
        for h in range(N_REC_HEADS):
            ks_, _ = key_lanes(h)
            st = st_ref[h]
            states.append([])
            for u in range(REC_UNROLL):
                states[h].append(st)
                st_old[u, h] = st
                decay_col = jnp.broadcast_to(facs[u][5][:, ks_], (HEAD_LANES, HEAD_LANES)).T
                st = st * decay_col + incs[u][h]
            st_ref[h] = st
        for u, rows in enumerate(all_rows):
            for h in range(N_REC_HEADS):
                emit(h, rows, chunk_output(h, a16[u][h], facs[u][3], vs[u][h], states[h][u]))

        w_min = functools.reduce(jnp.minimum, ws)

        @pl.when(jnp.min(w_min) < -DECAY_SPAN_LIMIT)
        def _():
            for u, rows in enumerate(all_rows):
                lhs, _, ks16, qe16, _, _ = chunk_factors(qbuf[u], kbuf[u], wbuf[u], exact=True)
                for h in range(N_REC_HEADS):
                    hs = slice(h * HEAD_LANES, (h + 1) * HEAD_LANES)
                    a = head_scores(h, lhs, None, ks16, exact_same_block_scores(u, h))
                    emit(h, rows, chunk_output(h, a, qe16, v_ref[0, rows, hs], st_old[u, h]))

        return carry

    lax.fori_loop(0, n_chunks // REC_UNROLL, group_body, 0)


def _scan_mask(chunk, block, reverse):
    i = np.arange(chunk)
    same = (i[:, None] // block) == (i[None, :] // block)
    seen = (i[None, :] >= i[:, None]) if reverse else (i[None, :] <= i[:, None])
    return jnp.asarray((same & seen).astype(np.float32), BF16)


def _recurrent_pass(mode, reverse, final, chunk, block, p3, bsz, t, extra, prev=None, norm_w=None):
    n = bsz * t
    tb = REC_TOKENS_PER_STEP
    nblk = t // tb
    width = N_REC_HEADS * HEAD_LANES
    key_width = N_REC_HEADS * (GLA_DK if mode == "gla" else HEAD_LANES)

    def tok(b, i):
        return b * nblk + ((nblk - 1 - i) if reverse else i)

    def chunk_spec(ch):
        return pl.BlockSpec((1, tb, PCHUNK), lambda b, i: (ch, tok(b, i), 0))

    tok_spec = lambda w: pl.BlockSpec((tb, w), lambda b, i: (tok(b, i), 0))
    operands, specs = [], []
    if mode == "gla":
        lr, up_hi, up_lo, gbias = extra
        operands += [p3, p3, lr, up_hi, up_lo, gbias]
        specs += [chunk_spec(CH_GLA_QK), chunk_spec(CH_GLA_V), tok_spec(LR_PAD),
                  _resident(up_hi.shape), _resident(up_lo.shape), _resident(gbias.shape)]
        gate_ch = CH_GLA_G
    else:
        (lb,) = extra
        operands += [p3, p3, p3, lb]
        specs += [chunk_spec(CH_HG_Q), chunk_spec(CH_HG_FB if reverse else CH_HG_FF), chunk_spec(CH_HG_I),
                  _resident(lb.shape)]
        gate_ch = CH_HG_G
    tri = _scan_mask(chunk, block, reverse)
    operands.append(tri)
    specs.append(_resident(tri.shape))
    if final:
        operands += [prev, p3, norm_w]
        specs += [tok_spec(width), chunk_spec(gate_ch), _resident(norm_w.shape)]
    out_dtype = BF16 if final else F32
    return pl.pallas_call(
        functools.partial(_rec_kernel, mode=mode, reverse=reverse, final=final, chunk=chunk, block=block),
        grid=(bsz, nblk),
        in_specs=specs,
        out_specs=tok_spec(width),
        out_shape=jax.ShapeDtypeStruct((n, width), out_dtype),
        scratch_shapes=[
            pltpu.VMEM((N_REC_HEADS, HEAD_LANES, HEAD_LANES), F32),
            pltpu.VMEM((REC_UNROLL, N_REC_HEADS, HEAD_LANES, HEAD_LANES), F32),
            pltpu.VMEM((REC_UNROLL, chunk, key_width), F32),
            pltpu.VMEM((REC_UNROLL, chunk, key_width), F32),
            pltpu.VMEM((REC_UNROLL, chunk, key_width), F32),
        ],
        compiler_params=_cparams(("arbitrary", "arbitrary")),
        name=f"{mode}_{'bwd' if reverse else 'fwd'}",
    )(*operands)


GLA_CHUNK, GLA_BLOCK = 128, 128
HGRN_CHUNK, HGRN_BLOCK = 64, 16


def _gla_branch(p3, lr, up_pads, gbias_pads, norm_w, bsz, t):
    fwd = _recurrent_pass("gla", False, False, GLA_CHUNK, GLA_BLOCK, p3, bsz, t,
                          (lr, up_pads[0][0], up_pads[0][1], gbias_pads[0]))
    return _recurrent_pass("gla", True, True, GLA_CHUNK, GLA_BLOCK, p3, bsz, t,
                           (lr, up_pads[1][0], up_pads[1][1], gbias_pads[1]), prev=fwd, norm_w=norm_w)


def _hgrn_branch(p3, lbs, norm_w, bsz, t):
    fwd = _recurrent_pass("hgrn", False, False, HGRN_CHUNK, HGRN_BLOCK, p3, bsz, t, (lbs[0],))
    return _recurrent_pass("hgrn", True, True, HGRN_CHUNK, HGRN_BLOCK, p3, bsz, t, (lbs[1],),
                           prev=fwd, norm_w=norm_w)


MLP_ROWS = 512
MLP_GROUP_ROWS = 256


def _mlp_kernel(x_ref, ona_ref, ogla_ref, ohg_ref, gates_ref, mod_ref, wna_ref, wgla_ref, whg_ref, wout_ref,
                n2_ref, wg_ref, wu_ref, wd_ref, fn_ref, o_ref, *, last):
    d = D_MODEL
    gate1 = mod_ref[0, :, 2 * d:3 * d]
    shift2 = mod_ref[0, :, 3 * d:4 * d]
    scale2 = mod_ref[0, :, 4 * d:5 * d]
    gate2 = mod_ref[0, :, 5 * d:6 * d]
    groups = [slice(i * MLP_GROUP_ROWS, (i + 1) * MLP_GROUP_ROWS) for i in range(MLP_ROWS // MLP_GROUP_ROWS)]

    def gate(rs, i):
        g = jnp.concatenate([gates_ref[2 * i, rs, :].astype(F32), gates_ref[2 * i + 1, rs, :].astype(F32)], axis=1)
        return _sigmoid_pair(g)[0]

    merged = [(gate(rs, 0) * _dot(ona_ref[rs, :], wna_ref[0])
               + gate(rs, 1) * _dot(ogla_ref[rs, :], wgla_ref[0])
               + gate(rs, 2) * _dot(ohg_ref[rs, :], whg_ref[0])).astype(BF16) for rs in groups]
    x1 = [x_ref[rs, :] + gate1 * _dot(m, wout_ref[0]) for rs, m in zip(groups, merged)]
    h = [_rmsnorm_mod(x, n2_ref[0], scale2, shift2).astype(BF16) for x in x1]
    act = [(_silu(_dot(hh, wg_ref[0])) * _dot(hh, wu_ref[0])).astype(BF16) for hh in h]
    for rs, x, a in zip(groups, x1, act):
        x2 = x + gate2 * _dot(a, wd_ref[0])
        if last:
            ms = jnp.mean(x2 * x2, axis=-1, keepdims=True)
            x2 = x2 * lax.rsqrt(ms + RMS_EPS) * fn_ref[...]
        o_ref[rs, :] = x2


def _merge_mlp(x2, o_na, o_gla, o_hg, p3, mod_l, layer_wts, final_norm_w, layer, tokens_per_batch, last):
    n, d = x2.shape
    tm = MLP_ROWS
    per_batch = tokens_per_batch // tm
    row = lambda w: pl.BlockSpec((tm, w), lambda i: (i, 0))
    return pl.pallas_call(
        functools.partial(_mlp_kernel, last=last),
        grid=(n // tm,),
        in_specs=[
            row(d), row(NA_WIDTH), row(NA_WIDTH), row(NA_WIDTH),
            pl.BlockSpec((N_GATE_CHUNKS, tm, PCHUNK), lambda i: (CH_GATES // N_GATE_CHUNKS, i, 0)),
            pl.BlockSpec((1, 1, N_MOD * d), lambda i: (i // per_batch, 0, 0)),
        ] + [_layer_resident(w.shape, layer) for w in layer_wts] + [_resident(final_norm_w.shape)],
        out_specs=row(d),
        out_shape=jax.ShapeDtypeStruct((n, d), F32),
        compiler_params=_cparams(("arbitrary",)),
        name="merge_mlp",
    )(x2, o_na, o_gla, o_hg, p3, mod_l, *layer_wts, final_norm_w)


def _split_w_in(w):
    head = N_HEAD_CHUNKS * PCHUNK
    tail = head + 2 * GLA_RANK
    assert w.shape[2] - tail == N_TAIL_CHUNKS * PCHUNK
    w_lr = jnp.pad(w[:, :, head:tail], ((0, 0), (0, 0), (0, LR_PAD - 2 * GLA_RANK)))
    return w[:, :, :head].astype(BF16), w[:, :, tail:].astype(BF16), w_lr.astype(BF16)


def _query_scales():
    qs = np.ones((N_HEAD_CHUNKS, 1, PCHUNK), np.float32)
    qs[CH_NA_Q] = NA_HEAD_DIM ** -0.5
    qs[CH_GLA_QK, :, :GLA_HEADS * GLA_DK] = GLA_DK ** -0.5
    return jnp.asarray(qs)


def kernel(x, c, w_ada, b_ada, norm1_w, w_in, na_rpb, gla_lr_up, gla_lr_bias, gla_norm_w, hgrn_lb_logits, hgrn_norm_w, w_proj_na, w_proj_gla, w_proj_hgrn, w_out, norm2_w, w_ffn_gate, w_ffn_up, w_ffn_down, final_norm_w):
    bsz, t, d = x.shape
    depth = w_in.shape[0]
    assert d == D_MODEL and t % REC_TOKENS_PER_STEP == 0 and t % INPROJ_ROWS == 0
    n = bsz * t

    mods = _modulation(c, w_ada, b_ada).reshape(depth, bsz, 1, N_MOD * d)
    lb_p = jax.nn.softmax(hgrn_lb_logits.astype(F32), axis=0)
    lb_all = jnp.clip(jnp.cumsum(lb_p, axis=0) - lb_p[0], 0.0, 1.0)

    w_head, w_tail, w_lr = _split_w_in(w_in)
    q_scale = _query_scales()
    bias_tabs = _na_bias_tables(na_rpb, t // GRID_W)
    norm1 = norm1_w.reshape(depth, 1, d)
    mlp_wts = [w_proj_na.astype(BF16), w_proj_gla.astype(BF16), w_proj_hgrn.astype(BF16), w_out.astype(BF16),
               norm2_w.reshape(depth, 1, d), w_ffn_gate.astype(BF16), w_ffn_up.astype(BF16),
               w_ffn_down.astype(BF16)]
    up = jnp.zeros((depth, 2, LR_PAD, GLA_HEADS * GLA_DK), F32)
    for s in range(2):
        up = up.at[:, s, s * GLA_RANK:(s + 1) * GLA_RANK].set(gla_lr_up[:, s])
    up_hi = up.astype(BF16)
    up_lo = (up - up_hi.astype(F32)).astype(BF16)

    xf = x.reshape(n, d)
    for l in range(depth):
        p3, lr = _in_projection(xf, mods[l], norm1, q_scale, w_head, w_tail, w_lr, l, t)

        o_na = _neighborhood_attention(p3, bias_tabs, l, bsz, t)

        up_pads = [(up_hi[l, s], up_lo[l, s]) for s in range(2)]
        gbias = [gla_lr_bias[l, s].reshape(1, -1) for s in range(2)]
        o_gla = _gla_branch(p3, lr, up_pads, gbias, gla_norm_w[l].reshape(1, -1), bsz, t)

        lbs = [lb_all[l, s * HGRN_KEY_WIDTH:(s + 1) * HGRN_KEY_WIDTH].reshape(1, -1) for s in range(2)]
        o_hg = _hgrn_branch(p3, lbs, hgrn_norm_w[l].reshape(1, -1), bsz, t)

        xf = _merge_mlp(xf, o_na, o_gla, o_hg, p3, mods[l], mlp_wts, final_norm_w.reshape(1, d), l, t,
                        last=(l == depth - 1))
    return xf.reshape(bsz, t, d)
```

```python
import functools

import jax
import jax.numpy as jnp
import numpy as np
from jax import lax
from jax.experimental import pallas as pl
from jax.experimental.pallas import tpu as pltpu

F32 = jnp.float32
BF16 = jnp.bfloat16

D_MODEL = 1024
GRID_W = 64
WIN_R = 8
WIN_C = 16
NA_HEADS = 8
NA_HEAD_DIM = 64
NA_WIDTH = 512
GLA_HEADS = 4
GLA_DK = 64
GLA_RANK = 16
GLA_TAU = 16.0
HGRN_KEY_WIDTH = 512
N_MOD = 6
FFN_HIDDEN = 2816
RMS_EPS = 1e-6
NEG_INF = -1e30
F_FLOOR = 1e-30

PCHUNK = 512
N_GATE_CHUNKS = 6
N_HEAD_CHUNKS = 6
N_TAIL_CHUNKS = 11
CH_NA_Q, CH_NA_K, CH_NA_V, CH_GLA_QK, CH_GLA_V, CH_GLA_G = range(N_HEAD_CHUNKS)
CH_GATES = N_HEAD_CHUNKS
CH_HG_Q, CH_HG_FF, CH_HG_FB, CH_HG_I, CH_HG_G = range(CH_GATES + N_GATE_CHUNKS, CH_GATES + N_GATE_CHUNKS + 5)
N_PCHUNKS = N_HEAD_CHUNKS + N_TAIL_CHUNKS
LR_PAD = 128
HEAD_LANES = 128
SUBLANES = 8
N_REC_HEADS = 4

VMEM_LIMIT_BYTES = 56 * 1024 * 1024

REC_TOKENS_PER_STEP = 512
REC_UNROLL = 4
DECAY_SPAN_LIMIT = 60.0


def _cparams(sem):
    return pltpu.CompilerParams(dimension_semantics=sem, vmem_limit_bytes=VMEM_LIMIT_BYTES)


def _resident(shape):
    nd = len(shape)
    return pl.BlockSpec(shape, lambda *_: (0,) * nd, pipeline_mode=pl.Buffered(1))


def _layer_resident(shape, layer):
    nd = len(shape)
    return pl.BlockSpec((1,) + tuple(shape[1:]), lambda *_: (layer,) + (0,) * (nd - 1), pipeline_mode=pl.Buffered(1))


def _dot(a, b):
    return jnp.dot(a, b, preferred_element_type=F32)


def _dot_nt(a, b):
    return lax.dot_general(a, b, (((1,), (1,)), ((), ())), preferred_element_type=F32)


def _dot_tn(a, b):
    return lax.dot_general(a, b, (((0,), (0,)), ((), ())), preferred_element_type=F32)


def _sigmoid_pair(z):
    e = jnp.exp(-jnp.abs(z))
    r = 1.0 / (1.0 + e)
    er = e * r
    pos = z >= 0
    return jnp.where(pos, r, er), jnp.where(pos, er, r)


def _silu(z):
    return z * _sigmoid_pair(z)[0]


def _split_bf16(a):
    hi = a.astype(BF16)
    lo = (a - hi.astype(F32)).astype(BF16)
    return hi, lo


MOD_COLS = 1536


def _mod_kernel(c_ref, w_ref, b_ref, o_ref):
    c_act = _silu(c_ref[...])
    c_hi, c_lo = _split_bf16(c_act)
    w_hi, w_lo = _split_bf16(w_ref[0])
    acc = _dot(c_hi, w_hi) + _dot(c_lo, w_hi) + _dot(c_hi, w_lo)
    o_ref[0] = acc + b_ref[0]


def _modulation(c, w_ada, b_ada):
    depth, d, width = w_ada.shape
    bsz = c.shape[0]
    rows = 8
    c_pad = jnp.zeros((rows, d), F32).at[:bsz].set(c)
    out = pl.pallas_call(
        _mod_kernel,
        grid=(depth, width // MOD_COLS),
        in_specs=[
            pl.BlockSpec((rows, d), lambda l, j: (0, 0)),
            pl.BlockSpec((1, d, MOD_COLS), lambda l, j: (l, 0, j)),
            pl.BlockSpec((1, 1, MOD_COLS), lambda l, j: (l, 0, j)),
        ],
        out_specs=pl.BlockSpec((1, rows, MOD_COLS), lambda l, j: (l, 0, j)),
        out_shape=jax.ShapeDtypeStruct((depth, rows, width), F32),
        compiler_params=_cparams(("arbitrary", "arbitrary")),
        name="adaln_mod",
    )(c_pad, w_ada, b_ada.reshape(depth, 1, width))
    return out[:, :bsz]


INPROJ_ROWS = 512
INPROJ_NORM_AFTER = (1, 3)


def _rmsnorm_mod(x, w, scale, shift):
    ms = jnp.mean(x * x, axis=-1, keepdims=True)
    y = x * lax.rsqrt(ms + RMS_EPS) * w
    return y * (1.0 + scale) + shift


def _inproj_kernel(x_ref, xn_ref, mod_ref, modn_ref, nw_ref, qs_ref, wh_ref, wt_ref, wl_ref, p_ref, lr_ref, h_ref):
    d = D_MODEL
    step = pl.program_id(0)
    cur = step % 2

    def normed(xr, mr, rows):
        return _rmsnorm_mod(xr[rows, :], nw_ref[0], mr[0, :, d:2 * d], mr[0, :, 0:d]).astype(BF16)

    @pl.when(step == 0)
    def _():
        h_ref[0] = normed(x_ref, mod_ref, slice(None))

    half = INPROJ_ROWS // 2
    for j in range(N_PCHUNKS):
        if j < N_HEAD_CHUNKS:
            p_ref[j] = (_dot(h_ref[cur], wh_ref[0, :, j * PCHUNK:(j + 1) * PCHUNK]) * qs_ref[j]).astype(BF16)
        else:
            t = j - N_HEAD_CHUNKS
            dst = CH_HG_Q + t if t < N_TAIL_CHUNKS - N_GATE_CHUNKS else CH_GATES + t - (N_TAIL_CHUNKS - N_GATE_CHUNKS)
            p_ref[dst] = _dot(h_ref[cur], wt_ref[0, :, t * PCHUNK:(t + 1) * PCHUNK]).astype(BF16)
        if j in INPROJ_NORM_AFTER:
            rows = slice(0, half) if j == INPROJ_NORM_AFTER[0] else slice(half, INPROJ_ROWS)
            h_ref[1 - cur, rows, :] = normed(xn_ref, modn_ref, rows)
    lr_ref[...] = _dot(h_ref[cur], wl_ref[0]).astype(BF16)


def _in_projection(x2, mod_l, norm_w, q_scale, w_head, w_tail, w_lr, layer, tokens_per_batch):
    n, d = x2.shape
    tm = INPROJ_ROWS
    per_batch = tokens_per_batch // tm
    last = n // tm - 1
    nxt = lambda i: jnp.minimum(i + 1, last)
    return pl.pallas_call(
        _inproj_kernel,
        grid=(n // tm,),
        in_specs=[
            pl.BlockSpec((tm, d), lambda i: (i, 0)),
            pl.BlockSpec((tm, d), lambda i: (nxt(i), 0)),
            pl.BlockSpec((1, 1, N_MOD * d), lambda i: (i // per_batch, 0, 0)),
            pl.BlockSpec((1, 1, N_MOD * d), lambda i: (nxt(i) // per_batch, 0, 0)),
            _layer_resident(norm_w.shape, layer),
            _resident(q_scale.shape),
            _layer_resident(w_head.shape, layer),
            _layer_resident(w_tail.shape, layer),
            _layer_resident(w_lr.shape, layer),
        ],
        out_specs=[
            pl.BlockSpec((N_PCHUNKS, tm, PCHUNK), lambda i: (0, i, 0)),
            pl.BlockSpec((tm, LR_PAD), lambda i: (i, 0)),
        ],
        out_shape=[
            jax.ShapeDtypeStruct((N_PCHUNKS, n, PCHUNK), BF16),
            jax.ShapeDtypeStruct((n, LR_PAD), BF16),
        ],
        scratch_shapes=[pltpu.VMEM((2, tm, d), BF16)],
        compiler_params=_cparams(("arbitrary",)),
        name="in_proj",
    )(x2, x2, mod_l, mod_l, norm_w, q_scale, w_head, w_tail, w_lr)


NA_ROWS_PER_STEP = 8
NA_ROW_UNROLL = 4
NA_BAND = WIN_R * GRID_W
NA_VARIANTS = 8


def _na_kernel(q_ref, k_ref, v_ref, bias_ref, o_ref, *, rows_total):
    step = pl.program_id(1)
    lane = lax.broadcasted_iota(jnp.int32, (GRID_W, HEAD_LANES), 1)
    low = lane < NA_HEAD_DIM

    def row_group(gi, carry):
        items = []
        for u in range(NA_ROW_UNROLL):
            rr = gi * NA_ROW_UNROLL + u
            r = step * NA_ROWS_PER_STEP + rr
            row_start = jnp.clip(r - WIN_R // 2, 0, rows_total - WIN_R)
            variant = jnp.where(r < WIN_R // 2, r,
                                jnp.where(r > rows_total - WIN_R // 2, r - (rows_total - WIN_R), WIN_R // 2))
            k0 = pl.multiple_of(row_start * GRID_W, GRID_W)
            q0 = pl.multiple_of(rr * GRID_W, GRID_W)
            for hp in range(NA_HEADS // 2):
                items.append((q0, k0, variant, hp, slice(hp * HEAD_LANES, (hp + 1) * HEAD_LANES)))
        scores = []
        for q0, k0, variant, hp, hs in items:
            qp = q_ref[0, pl.ds(q0, GRID_W), hs]
            zero = jnp.zeros_like(qp)
            q2 = jnp.concatenate([jnp.where(low, qp, zero), jnp.where(low, zero, qp)], axis=0)
            scores.append(_dot_nt(q2, k_ref[0, pl.ds(k0, NA_BAND), hs]) + bias_ref[0, variant, hp])
        probs = []
        for s in scores:
            e = jnp.exp(s - jnp.max(s, axis=-1, keepdims=True))
            probs.append((e.astype(BF16), jnp.sum(e, axis=-1, keepdims=True)))
        for (q0, k0, variant, hp, hs), (e16, denom) in zip(items, probs):
            o2 = _dot(e16, v_ref[0, pl.ds(k0, NA_BAND), hs]) / denom
            o_ref[pl.ds(q0, GRID_W), hs] = jnp.where(low, o2[:GRID_W], o2[GRID_W:]).astype(BF16)
        return carry

    lax.fori_loop(0, NA_ROWS_PER_STEP // NA_ROW_UNROLL, row_group, 0)


def _na_bias_tables(rpb, rows):
    depth = rpb.shape[0]
    wr = min(WIN_R, rows)
    col = np.arange(GRID_W)
    col_start = np.clip(col - WIN_C // 2, 0, GRID_W - WIN_C)
    col_mask = (col[None, :] >= col_start[:, None]) & (col[None, :] < col_start[:, None] + WIN_C)
    col_off = np.clip(col[None, :] - col[:, None], -(WIN_C - 1), WIN_C - 1) + (WIN_C - 1)
    n_ro, n_co = 2 * WIN_R - 1, 2 * WIN_C - 1
    onehot = (col_off.reshape(-1)[None, :] == np.arange(n_co)[:, None]).astype(np.float32)
    by_col = jnp.dot(rpb.astype(F32).reshape(depth * NA_HEADS * n_ro, n_co), onehot, precision=lax.Precision.HIGHEST)
    by_col = by_col.reshape(depth, NA_HEADS, n_ro, GRID_W, GRID_W)
    by_col = jnp.where(col_mask[None, None, None], by_col, NEG_INF)
    rep_rows = list(range(WIN_R // 2)) + [WIN_R // 2] + [rows - WIN_R + v for v in range(WIN_R // 2 + 1, WIN_R)]
    tabs = []
    for r in rep_rows:
        row_start = int(np.clip(r - WIN_R // 2, 0, rows - wr))
        ro0 = row_start - r + (WIN_R - 1)
        tabs.append(by_col[:, :, ro0:ro0 + wr])
    tabs = jnp.stack(tabs, axis=1).transpose(0, 1, 2, 4, 3, 5)
    return tabs.reshape(depth, NA_VARIANTS, NA_HEADS // 2, 2 * GRID_W, wr * GRID_W)


def _neighborhood_attention(p3, bias_tabs, layer, bsz, t):
    rows = t // GRID_W
    assert rows >= 2 * WIN_R and rows % NA_ROWS_PER_STEP == 0
    n = bsz * t
    tq = NA_ROWS_PER_STEP * GRID_W
    steps = t // tq
    return pl.pallas_call(
        functools.partial(_na_kernel, rows_total=rows),
        grid=(bsz, steps),
        in_specs=[
            pl.BlockSpec((1, tq, PCHUNK), lambda b, i: (CH_NA_Q, b * steps + i, 0)),
            pl.BlockSpec((1, t, PCHUNK), lambda b, i: (CH_NA_K, b, 0)),
            pl.BlockSpec((1, t, PCHUNK), lambda b, i: (CH_NA_V, b, 0)),
            _layer_resident(bias_tabs.shape, layer),
        ],
        out_specs=pl.BlockSpec((tq, NA_WIDTH), lambda b, i: (b * steps + i, 0)),
        out_shape=jax.ShapeDtypeStruct((n, NA_WIDTH), BF16),
        compiler_params=_cparams(("arbitrary", "arbitrary")),
        name="neighborhood_attn",
    )(p3, p3, p3, bias_tabs)


def _rec_kernel(*refs, mode, reverse, final, chunk, block):
    it = iter(refs)
    q_ref = next(it)
    if mode == "gla":
        v_ref, lr_ref, uph_ref, upl_ref, gbias_ref = (next(it) for _ in range(5))
    else:
        f_ref, v_ref, lb_ref = (next(it) for _ in range(3))
    tri_ref = next(it)
    if final:
        prev_ref, gate_ref, nw_ref = (next(it) for _ in range(3))
    o_ref = next(it)
    st_ref, st_old, kbuf, wbuf, qbuf = (next(it) for _ in range(5))

    c, s = chunk, block
    nb = c // s
    dk = GLA_DK if mode == "gla" else HEAD_LANES
    width = N_REC_HEADS * dk
    n_chunks = REC_TOKENS_PER_STEP // c

    def key_lanes(h):
        if dk == HEAD_LANES:
            return slice(h * HEAD_LANES, (h + 1) * HEAD_LANES), None
        t = (h * dk) // HEAD_LANES
        lane = lax.broadcasted_iota(jnp.int32, (1, HEAD_LANES), 1)
        mine = (lane // dk) == (h - t * (HEAD_LANES // dk))
        return slice(t * HEAD_LANES, (t + 1) * HEAD_LANES), mine

    def own(x, mine):
        return x if mine is None else jnp.where(mine, x, jnp.zeros_like(x))

    @pl.when(pl.program_id(1) == 0)
    def _():
        st_ref[...] = jnp.zeros_like(st_ref)

    ii = lax.broadcasted_iota(jnp.int32, (c, c), 0)
    jj = lax.broadcasted_iota(jnp.int32, (c, c), 1)
    bi, bj = ii // s, jj // s
    gap = (bj - bi) if reverse else (bi - bj)
    seen = (jj >= ii) if reverse else (jj <= ii)
    cls = jnp.where(gap == 0, jnp.where(seen, 0, -1), jnp.where(gap > 0, gap, -1))
    row_id = lax.broadcasted_iota(jnp.int32, (c, 1), 0)
    col_id = lax.broadcasted_iota(jnp.int32, (1, c), 1)

    def prev_block(i, steps=1):
        return i + steps if reverse else i - steps

    def per_block(vals):
        return jnp.concatenate([jnp.broadcast_to(x, (s, width)) for x in vals], axis=0)

    def chunk_inputs(rows):
        if mode == "gla":
            q = q_ref[0, rows, :width].astype(F32)
            k = q_ref[0, rows, width:].astype(F32)
            lr = lr_ref[rows, :]
            z = _dot(lr, uph_ref[...]) + _dot(lr, upl_ref[...]) + gbias_ref[...]
            g = (jnp.minimum(z, 0.0) - jnp.log(1.0 + jnp.exp(-jnp.abs(z)))) * (1.0 / GLA_TAU)
        else:
            z = f_ref[0, rows, :].astype(F32)
            sig, sig_neg = _sigmoid_pair(z)
            lb = lb_ref[...]
            g = jnp.log(jnp.maximum(lb + (1.0 - lb) * sig, F_FLOOR))
            k = (1.0 - lb) * sig_neg
            q = _silu(q_ref[0, rows, :].astype(F32))
        return q, k, g

    def chunk_factors(q, k, w, exact):
        w3 = w.reshape(nb, s, width)
        last = 0 if reverse else s - 1
        wb = [w3[i, last:last + 1, :] for i in range(nb)]
        order = list(range(nb - 1, -1, -1)) if reverse else list(range(nb))
        beta = [None] * nb
        run = jnp.zeros((1, width), F32)
        for i in order:
            beta[i] = run
            run = run + wb[i]
        total = run
        ewb = [jnp.exp(x) for x in wb]
        qd = q * jnp.exp(w)
        ks = k * jnp.exp(per_block(wb) - w)
        kd16 = None if exact else (k * jnp.exp(-w)).astype(BF16)
        one = jnp.ones((1, width), F32)
        lhs, fac = [qd.astype(BF16)], [one] * nb
        for dgap in range(2, nb):
            fac = [fac[i] * (ewb[prev_block(i, dgap - 1)] if 0 <= prev_block(i, dgap - 1) < nb else one)
                   for i in range(nb)]
            lhs.append((qd * per_block(fac)).astype(BF16))
        ks16 = ks.astype(BF16)
        if nb == 1:
            return lhs, kd16, ks16, lhs[0], ks16, jnp.exp(total)
        qe16 = (qd * jnp.exp(per_block(beta))).astype(BF16)
        ke16 = (ks * jnp.exp(per_block([total - beta[i] - wb[i] for i in range(nb)]))).astype(BF16)
        return lhs, kd16, ks16, qe16, ke16, jnp.exp(total)

    def head_scores(h, lhs, kd16, ks16, sd):
        ks_, mine = key_lanes(h)
        if sd is None:
            sd = _dot_nt(own(lhs[0][:, ks_], mine), kd16[:, ks_])
        a = jnp.where(cls == 0, sd, 0.0)
        if nb > 1:
            def far_rows(dgap):
                return slice(0, c - dgap * s) if reverse else slice(dgap * s, c)

            parts = [own(lhs[max(dgap - 1, 0)][:, ks_], mine)[far_rows(dgap)] for dgap in range(1, nb)]
            so = _dot_nt(jnp.concatenate(parts, axis=0), ks16[:, ks_])
            start = 0
            for dgap in range(1, nb):
                n_far = c - dgap * s
                blank = jnp.zeros((dgap * s, c), F32)
                part = so[start:start + n_far]
                start += n_far
                full = jnp.concatenate([part, blank] if reverse else [blank, part], axis=0)
                a = jnp.where(cls == dgap, full, a)
        return a.astype(BF16)

    def chunk_output(h, a16, qe16, v16, st):
        ks_, mine = key_lanes(h)
        lhs = jnp.concatenate([own(qe16[:, ks_], mine), a16], axis=1)
        return _dot(lhs, jnp.concatenate([st.astype(BF16), v16], axis=0))

    def emit(h, rows, o):
        hs = slice(h * HEAD_LANES, (h + 1) * HEAD_LANES)
        if final:
            tot = prev_ref[rows, hs] + o
            ms = jnp.mean(tot * tot, axis=-1, keepdims=True)
            y = tot * lax.rsqrt(ms + RMS_EPS) * nw_ref[...]
            o_ref[rows, hs] = (y * _silu(gate_ref[0, rows, hs].astype(F32))).astype(o_ref.dtype)
        else:
            o_ref[rows, hs] = o

    def exact_same_block_scores(u, h):
        hs, mine = key_lanes(h)

        def col_body(jg, acc):
            j0 = pl.multiple_of(jg * SUBLANES, SUBLANES)
            k8 = kbuf[u, pl.ds(j0, SUBLANES), hs]
            w8 = wbuf[u, pl.ds(j0, SUBLANES), hs]
            for jr in range(SUBLANES):
                j = j0 + jr
                same = (row_id // s) == (j // s)
                valid = same & ((row_id <= j) if reverse else (row_id >= j))
                e = jnp.exp(jnp.where(valid, wbuf[u, :, hs] - w8[jr:jr + 1], 0.0))
                t = jnp.sum(own(qbuf[u, :, hs], mine) * k8[jr:jr + 1] * e, axis=-1, keepdims=True)
                acc = jnp.where(valid & (col_id == j), t, acc)
            return acc

        return lax.fori_loop(0, c // SUBLANES, col_body, jnp.zeros((c, c), F32))

    def group_body(gi, carry):
        all_rows = []
        for u in range(REC_UNROLL):
            ci = gi * REC_UNROLL + u
            cc = (n_chunks - 1 - ci) if reverse else ci
            all_rows.append(pl.ds(pl.multiple_of(cc * c, c), c))
        ws, facs = [], []
        for u, rows in enumerate(all_rows):
            q, k, g = chunk_inputs(rows)
            g_hi, g_lo = _split_bf16(g)
            w2 = _dot(tri_ref[...], jnp.concatenate([g_hi, g_lo], axis=1))
            w = w2[:, :width] + w2[:, width:]
            ws.append(w)
            facs.append(chunk_factors(q, k, w, exact=False))
            qbuf[u] = q
            kbuf[u] = k
            wbuf[u] = w
        a16 = [[head_scores(h, f[0], f[1], f[2], None) for h in range(N_REC_HEADS)] for f in facs]
        vs, incs = [], []
        for u, rows in enumerate(all_rows):
            ke16 = facs[u][4]
            vs.append([v_ref[0, rows, h * HEAD_LANES:(h + 1) * HEAD_LANES] for h in range(N_REC_HEADS)])
            incs.append([_dot_tn(own(ke16[:, key_lanes(h)[0]], key_lanes(h)[1]), vs[u][h])
                         for h in range(N_REC_HEADS)])
        states = []
        for h in range(N_REC_HEADS):
            ks_, _ = key_lanes(h)
            st = st_ref[h]
            states.append([])
            for u in range(REC_UNROLL):
                states[h].append(st)
                st_old[u, h] = st
                decay_col = jnp.broadcast_to(facs[u][5][:, ks_], (HEAD_LANES, HEAD_LANES)).T
                st = st * decay_col + incs[u][h]
            st_ref[h] = st
        for u, rows in enumerate(all_rows):
            for h in range(N_REC_HEADS):
                emit(h, rows, chunk_output(h, a16[u][h], facs[u][3], vs[u][h], states[h][u]))

        w_min = functools.reduce(jnp.minimum, ws)

        @pl.when(jnp.min(w_min) < -DECAY_SPAN_LIMIT)
        def _():
            for u, rows in enumerate(all_rows):
                lhs, _, ks16, qe16, _, _ = chunk_factors(qbuf[u], kbuf[u], wbuf[u], exact=True)
                for h in range(N_REC_HEADS):
                    hs = slice(h * HEAD_LANES, (h + 1) * HEAD_LANES)
                    a = head_scores(h, lhs, None, ks16, exact_same_block_scores(u, h))
                    emit(h, rows, chunk_output(h, a, qe16, v_ref[0, rows, hs], st_old[u, h]))

        return carry

    lax.fori_loop(0, n_chunks // REC_UNROLL, group_body, 0)


def _scan_mask(chunk, block, reverse):
    i = np.arange(chunk)
    same = (i[:, None] // block) == (i[None, :] // block)
    seen = (i[None, :] >= i[:, None]) if reverse else (i[None, :] <= i[:, None])
    return jnp.asarray((same & seen).astype(np.float32), BF16)


def _recurrent_pass(mode, reverse, final, chunk, block, p3, bsz, t, extra, prev=None, norm_w=None):
    n = bsz * t
    tb = REC_TOKENS_PER_STEP
    nblk = t // tb
    width = N_REC_HEADS * HEAD_LANES
    key_width = N_REC_HEADS * (GLA_DK if mode == "gla" else HEAD_LANES)

    def tok(b, i):
        return b * nblk + ((nblk - 1 - i) if reverse else i)

    def chunk_spec(ch):
        return pl.BlockSpec((1, tb, PCHUNK), lambda b, i: (ch, tok(b, i), 0))

    tok_spec = lambda w: pl.BlockSpec((tb, w), lambda b, i: (tok(b, i), 0))
    operands, specs = [], []
    if mode == "gla":
        lr, up_hi, up_lo, gbias = extra
        operands += [p3, p3, lr, up_hi, up_lo, gbias]
        specs += [chunk_spec(CH_GLA_QK), chunk_spec(CH_GLA_V), tok_spec(LR_PAD),
                  _resident(up_hi.shape), _resident(up_lo.shape), _resident(gbias.shape)]
        gate_ch = CH_GLA_G
    else:
        (lb,) = extra
        operands += [p3, p3, p3, lb]
        specs += [chunk_spec(CH_HG_Q), chunk_spec(CH_HG_FB if reverse else CH_HG_FF), chunk_spec(CH_HG_I),
                  _resident(lb.shape)]
        gate_ch = CH_HG_G
    tri = _scan_mask(chunk, block, reverse)
    operands.append(tri)
    specs.append(_resident(tri.shape))
    if final:
        operands += [prev, p3, norm_w]
        specs += [tok_spec(width), chunk_spec(gate_ch), _resident(norm_w.shape)]
    out_dtype = BF16 if final else F32
    return pl.pallas_call(
        functools.partial(_rec_kernel, mode=mode, reverse=reverse, final=final, chunk=chunk, block=block),
        grid=(bsz, nblk),
        in_specs=specs,
        out_specs=tok_spec(width),
        out_shape=jax.ShapeDtypeStruct((n, width), out_dtype),
        scratch_shapes=[
            pltpu.VMEM((N_REC_HEADS, HEAD_LANES, HEAD_LANES), F32),
            pltpu.VMEM((REC_UNROLL, N_REC_HEADS, HEAD_LANES, HEAD_LANES), F32),
            pltpu.VMEM((REC_UNROLL, chunk, key_width), F32),
            pltpu.VMEM((REC_UNROLL, chunk, key_width), F32),
            pltpu.VMEM((REC_UNROLL, chunk, key_width), F32),
        ],
        compiler_params=_cparams(("arbitrary", "arbitrary")),
        name=f"{mode}_{'bwd' if reverse else 'fwd'}",
    )(*operands)


GLA_CHUNK, GLA_BLOCK = 128, 128
HGRN_CHUNK, HGRN_BLOCK = 64, 16


def _gla_branch(p3, lr, up_pads, gbias_pads, norm_w, bsz, t):
    fwd = _recurrent_pass("gla", False, False, GLA_CHUNK, GLA_BLOCK, p3, bsz, t,
                          (lr, up_pads[0][0], up_pads[0][1], gbias_pads[0]))
    return _recurrent_pass("gla", True, True, GLA_CHUNK, GLA_BLOCK, p3, bsz, t,
                           (lr, up_pads[1][0], up_pads[1][1], gbias_pads[1]), prev=fwd, norm_w=norm_w)


def _hgrn_branch(p3, lbs, norm_w, bsz, t):
    fwd = _recurrent_pass("hgrn", False, False, HGRN_CHUNK, HGRN_BLOCK, p3, bsz, t, (lbs[0],))
    return _recurrent_pass("hgrn", True, True, HGRN_CHUNK, HGRN_BLOCK, p3, bsz, t, (lbs[1],),
                           prev=fwd, norm_w=norm_w)


MLP_ROWS = 512
MLP_GROUP_ROWS = 256


def _mlp_kernel(x_ref, ona_ref, ogla_ref, ohg_ref, gates_ref, mod_ref, wna_ref, wgla_ref, whg_ref, wout_ref,
                n2_ref, wg_ref, wu_ref, wd_ref, fn_ref, o_ref, *, last):
    d = D_MODEL
    gate1 = mod_ref[0, :, 2 * d:3 * d]
    shift2 = mod_ref[0, :, 3 * d:4 * d]
    scale2 = mod_ref[0, :, 4 * d:5 * d]
    gate2 = mod_ref[0, :, 5 * d:6 * d]
    groups = [slice(i * MLP_GROUP_ROWS, (i + 1) * MLP_GROUP_ROWS) for i in range(MLP_ROWS // MLP_GROUP_ROWS)]

    def gate(rs, i):
        g = jnp.concatenate([gates_ref[2 * i, rs, :].astype(F32), gates_ref[2 * i + 1, rs, :].astype(F32)], axis=1)
        return _sigmoid_pair(g)[0]

    merged = [(gate(rs, 0) * _dot(ona_ref[rs, :], wna_ref[0])
               + gate(rs, 1) * _dot(ogla_ref[rs, :], wgla_ref[0])
               + gate(rs, 2) * _dot(ohg_ref[rs, :], whg_ref[0])).astype(BF16) for rs in groups]
    x1 = [x_ref[rs, :] + gate1 * _dot(m, wout_ref[0]) for rs, m in zip(groups, merged)]
    h = [_rmsnorm_mod(x, n2_ref[0], scale2, shift2).astype(BF16) for x in x1]
    act = [(_silu(_dot(hh, wg_ref[0])) * _dot(hh, wu_ref[0])).astype(BF16) for hh in h]
    for rs, x, a in zip(groups, x1, act):
        x2 = x + gate2 * _dot(a, wd_ref[0])
        if last:
            ms = jnp.mean(x2 * x2, axis=-1, keepdims=True)
            x2 = x2 * lax.rsqrt(ms + RMS_EPS) * fn_ref[...]
        o_ref[rs, :] = x2


def _merge_mlp(x2, o_na, o_gla, o_hg, p3, mod_l, layer_wts, final_norm_w, layer, tokens_per_batch, last):
    n, d = x2.shape
    tm = MLP_ROWS
    per_batch = tokens_per_batch // tm
    row = lambda w: pl.BlockSpec((tm, w), lambda i: (i, 0))
    return pl.pallas_call(
        functools.partial(_mlp_kernel, last=last),
        grid=(n // tm,),
        in_specs=[
            row(d), row(NA_WIDTH), row(NA_WIDTH), row(NA_WIDTH),
            pl.BlockSpec((N_GATE_CHUNKS, tm, PCHUNK), lambda i: (CH_GATES // N_GATE_CHUNKS, i, 0)),
            pl.BlockSpec((1, 1, N_MOD * d), lambda i: (i // per_batch, 0, 0)),
        ] + [_layer_resident(w.shape, layer) for w in layer_wts] + [_resident(final_norm_w.shape)],
        out_specs=row(d),
        out_shape=jax.ShapeDtypeStruct((n, d), F32),
        compiler_params=_cparams(("arbitrary",)),
        name="merge_mlp",
    )(x2, o_na, o_gla, o_hg, p3, mod_l, *layer_wts, final_norm_w)


WCAST_ROWS = 128


def _wcast_kernel(w_ref, head_ref, tail_ref, lr_ref):
    head = N_HEAD_CHUNKS * PCHUNK
    tail = head + 2 * GLA_RANK
    head_ref[0] = w_ref[0, :, :head].astype(BF16)
    tail_ref[0] = w_ref[0, :, tail:].astype(BF16)
    lr = w_ref[0, :, head:tail]
    lr_ref[0] = jnp.concatenate([lr, jnp.zeros((lr.shape[0], LR_PAD - 2 * GLA_RANK), lr.dtype)], axis=1).astype(BF16)


def _split_w_in(w):
    depth, d, width = w.shape
    head = N_HEAD_CHUNKS * PCHUNK
    n_tail = width - head - 2 * GLA_RANK
    assert n_tail == N_TAIL_CHUNKS * PCHUNK
    blk = lambda cols: pl.BlockSpec((1, WCAST_ROWS, cols), lambda l, i: (l, i, 0))
    return pl.pallas_call(
        _wcast_kernel,
        grid=(depth, d // WCAST_ROWS),
        in_specs=[blk(width)],
        out_specs=[blk(head), blk(n_tail), blk(LR_PAD)],
        out_shape=[jax.ShapeDtypeStruct((depth, d, head), BF16), jax.ShapeDtypeStruct((depth, d, n_tail), BF16),
                   jax.ShapeDtypeStruct((depth, d, LR_PAD), BF16)],
        compiler_params=_cparams(("arbitrary", "arbitrary")),
        name="w_in_cast",
    )(w)


def _query_scales():
    qs = np.ones((N_HEAD_CHUNKS, 1, PCHUNK), np.float32)
    qs[CH_NA_Q] = NA_HEAD_DIM ** -0.5
    qs[CH_GLA_QK, :, :GLA_HEADS * GLA_DK] = GLA_DK ** -0.5
    return jnp.asarray(qs)


def kernel(x, c, w_ada, b_ada, norm1_w, w_in, na_rpb, gla_lr_up, gla_lr_bias, gla_norm_w, hgrn_lb_logits, hgrn_norm_w, w_proj_na, w_proj_gla, w_proj_hgrn, w_out, norm2_w, w_ffn_gate, w_ffn_up, w_ffn_down, final_norm_w):
    bsz, t, d = x.shape
    depth = w_in.shape[0]
    assert d == D_MODEL and t % REC_TOKENS_PER_STEP == 0 and t % INPROJ_ROWS == 0
    n = bsz * t

    mods = _modulation(c, w_ada, b_ada).reshape(depth, bsz, 1, N_MOD * d)
    lb_p = jax.nn.softmax(hgrn_lb_logits.astype(F32), axis=0)
    lb_all = jnp.clip(jnp.cumsum(lb_p, axis=0) - lb_p[0], 0.0, 1.0)

    w_head, w_tail, w_lr = _split_w_in(w_in)
    q_scale = _query_scales()
    bias_tabs = _na_bias_tables(na_rpb, t // GRID_W)
    norm1 = norm1_w.reshape(depth, 1, d)
    mlp_wts = [w_proj_na.astype(BF16), w_proj_gla.astype(BF16), w_proj_hgrn.astype(BF16), w_out.astype(BF16),
               norm2_w.reshape(depth, 1, d), w_ffn_gate.astype(BF16), w_ffn_up.astype(BF16),
               w_ffn_down.astype(BF16)]
    up = jnp.zeros((depth, 2, LR_PAD, GLA_HEADS * GLA_DK), F32)
    for s in range(2):
        up = up.at[:, s, s * GLA_RANK:(s + 1) * GLA_RANK].set(gla_lr_up[:, s])
    up_hi = up.astype(BF16)
    up_lo = (up - up_hi.astype(F32)).astype(BF16)

    xf = x.reshape(n, d)
    for l in range(depth):
        p3, lr = _in_projection(xf, mods[l], norm1, q_scale, w_head, w_tail, w_lr, l, t)

        o_na = _neighborhood_attention(p3, bias_tabs, l, bsz, t)

        up_pads = [(up_hi[l, s], up_lo[l, s]) for s in range(2)]
        gbias = [gla_lr_bias[l, s].reshape(1, -1) for s in range(2)]
        o_gla = _gla_branch(p3, lr, up_pads, gbias, gla_norm_w[l].reshape(1, -1), bsz, t)

        lbs = [lb_all[l, s * HGRN_KEY_WIDTH:(s + 1) * HGRN_KEY_WIDTH].reshape(1, -1) for s in range(2)]
        o_hg = _hgrn_branch(p3, lbs, hgrn_norm_w[l].reshape(1, -1), bsz, t)

        xf = _merge_mlp(xf, o_na, o_gla, o_hg, p3, mods[l], mlp_wts, final_norm_w.reshape(1, d), l, t,
                        last=(l == depth - 1))
    return xf.reshape(bsz, t, d)
```

```python
import functools

import jax
import jax.numpy as jnp
import numpy as np
from jax import lax
from jax.experimental import pallas as pl
from jax.experimental.pallas import tpu as pltpu

F32 = jnp.float32
BF16 = jnp.bfloat16

D_MODEL = 1024
GRID_W = 64
WIN_R = 8
WIN_C = 16
NA_HEADS = 8
NA_HEAD_DIM = 64
NA_WIDTH = 512
GLA_HEADS = 4
GLA_DK = 64
GLA_RANK = 16
GLA_TAU = 16.0
HGRN_KEY_WIDTH = 512
N_MOD = 6
FFN_HIDDEN = 2816
RMS_EPS = 1e-6
NEG_INF = -1e30
F_FLOOR = 1e-30

PCHUNK = 512
N_GATE_CHUNKS = 6
N_HEAD_CHUNKS = 6
N_TAIL_CHUNKS = 11
CH_NA_Q, CH_NA_K, CH_NA_V, CH_GLA_QK, CH_GLA_V, CH_GLA_G = range(N_HEAD_CHUNKS)
CH_GATES = N_HEAD_CHUNKS
CH_HG_Q, CH_HG_FF, CH_HG_FB, CH_HG_I, CH_HG_G = range(CH_GATES + N_GATE_CHUNKS, CH_GATES + N_GATE_CHUNKS + 5)
N_PCHUNKS = N_HEAD_CHUNKS + N_TAIL_CHUNKS
LR_PAD = 128
HEAD_LANES = 128
SUBLANES = 8
N_REC_HEADS = 4

VMEM_LIMIT_BYTES = 56 * 1024 * 1024

REC_TOKENS_PER_STEP = 512
REC_UNROLL = 4
DECAY_SPAN_LIMIT = 60.0


def _cparams(sem):
    return pltpu.CompilerParams(dimension_semantics=sem, vmem_limit_bytes=VMEM_LIMIT_BYTES)


def _resident(shape):
    nd = len(shape)
    return pl.BlockSpec(shape, lambda *_: (0,) * nd, pipeline_mode=pl.Buffered(1))


def _layer_resident(shape, layer):
    nd = len(shape)
    return pl.BlockSpec((1,) + tuple(shape[1:]), lambda *_: (layer,) + (0,) * (nd - 1), pipeline_mode=pl.Buffered(1))


def _dot(a, b):
    return jnp.dot(a, b, preferred_element_type=F32)


def _dot_nt(a, b):
    return lax.dot_general(a, b, (((1,), (1,)), ((), ())), preferred_element_type=F32)


def _dot_tn(a, b):
    return lax.dot_general(a, b, (((0,), (0,)), ((), ())), preferred_element_type=F32)


def _sigmoid_pair(z):
    e = jnp.exp(-jnp.abs(z))
    r = 1.0 / (1.0 + e)
    er = e * r
    pos = z >= 0
    return jnp.where(pos, r, er), jnp.where(pos, er, r)


def _sigmoid(z):
    return 0.5 * jnp.tanh(0.5 * z) + 0.5


def _silu(z):
    return z * _sigmoid(z)


def _split_bf16(a):
    hi = a.astype(BF16)
    lo = (a - hi.astype(F32)).astype(BF16)
    return hi, lo


MOD_COLS = 1536


def _mod_kernel(c_ref, w_ref, b_ref, o_ref):
    c_act = _silu(c_ref[...])
    c_hi, c_lo = _split_bf16(c_act)
    w_hi, w_lo = _split_bf16(w_ref[0])
    acc = _dot(c_hi, w_hi) + _dot(c_lo, w_hi) + _dot(c_hi, w_lo)
    o_ref[0] = acc + b_ref[0]


def _modulation(c, w_ada, b_ada):
    depth, d, width = w_ada.shape
    bsz = c.shape[0]
    rows = 8
    c_pad = jnp.zeros((rows, d), F32).at[:bsz].set(c)
    out = pl.pallas_call(
        _mod_kernel,
        grid=(depth, width // MOD_COLS),
        in_specs=[
            pl.BlockSpec((rows, d), lambda l, j: (0, 0)),
            pl.BlockSpec((1, d, MOD_COLS), lambda l, j: (l, 0, j)),
            pl.BlockSpec((1, 1, MOD_COLS), lambda l, j: (l, 0, j)),
        ],
        out_specs=pl.BlockSpec((1, rows, MOD_COLS), lambda l, j: (l, 0, j)),
        out_shape=jax.ShapeDtypeStruct((depth, rows, width), F32),
        compiler_params=_cparams(("arbitrary", "arbitrary")),
        name="adaln_mod",
    )(c_pad, w_ada, b_ada.reshape(depth, 1, width))
    return out[:, :bsz]


INPROJ_ROWS = 512
INPROJ_NORM_AFTER = (1, 3)


def _rmsnorm_mod(x, w, scale, shift):
    ms = jnp.mean(x * x, axis=-1, keepdims=True)
    y = x * lax.rsqrt(ms + RMS_EPS) * w
    return y * (1.0 + scale) + shift


def _inproj_kernel(x_ref, xn_ref, mod_ref, modn_ref, nw_ref, qs_ref, wh_ref, wt_ref, wl_ref, p_ref, lr_ref, h_ref):
    d = D_MODEL
    step = pl.program_id(0)
    cur = step % 2

    def normed(xr, mr, rows):
        return _rmsnorm_mod(xr[rows, :], nw_ref[0], mr[0, :, d:2 * d], mr[0, :, 0:d]).astype(BF16)

    @pl.when(step == 0)
    def _():
        h_ref[0] = normed(x_ref, mod_ref, slice(None))

    half = INPROJ_ROWS // 2
    for j in range(N_PCHUNKS):
        if j < N_HEAD_CHUNKS:
            p_ref[j] = (_dot_nt(h_ref[cur], wh_ref[0, j * PCHUNK:(j + 1) * PCHUNK, :]) * qs_ref[j]).astype(BF16)
        else:
            t = j - N_HEAD_CHUNKS
            dst = CH_HG_Q + t if t < N_TAIL_CHUNKS - N_GATE_CHUNKS else CH_GATES + t - (N_TAIL_CHUNKS - N_GATE_CHUNKS)
            p_ref[dst] = _dot_nt(h_ref[cur], wt_ref[0, t * PCHUNK:(t + 1) * PCHUNK, :]).astype(BF16)
        if j in INPROJ_NORM_AFTER:
            rows = slice(0, half) if j == INPROJ_NORM_AFTER[0] else slice(half, INPROJ_ROWS)
            h_ref[1 - cur, rows, :] = normed(xn_ref, modn_ref, rows)
    lr_ref[...] = _dot_nt(h_ref[cur], wl_ref[0]).astype(BF16)


def _in_projection(x2, mod_l, norm_w, q_scale, w_head, w_tail, w_lr, layer, tokens_per_batch):
    n, d = x2.shape
    tm = INPROJ_ROWS
    per_batch = tokens_per_batch // tm
    last = n // tm - 1
    nxt = lambda i: jnp.minimum(i + 1, last)
    return pl.pallas_call(
        _inproj_kernel,
        grid=(n // tm,),
        in_specs=[
            pl.BlockSpec((tm, d), lambda i: (i, 0)),
            pl.BlockSpec((tm, d), lambda i: (nxt(i), 0)),
            pl.BlockSpec((1, 1, N_MOD * d), lambda i: (i // per_batch, 0, 0)),
            pl.BlockSpec((1, 1, N_MOD * d), lambda i: (nxt(i) // per_batch, 0, 0)),
            _layer_resident(norm_w.shape, layer),
            _resident(q_scale.shape),
            _layer_resident(w_head.shape, layer),
            _layer_resident(w_tail.shape, layer),
            _layer_resident(w_lr.shape, layer),
        ],
        out_specs=[
            pl.BlockSpec((N_PCHUNKS, tm, PCHUNK), lambda i: (0, i, 0)),
            pl.BlockSpec((tm, LR_PAD), lambda i: (i, 0)),
        ],
        out_shape=[
            jax.ShapeDtypeStruct((N_PCHUNKS, n, PCHUNK), BF16),
            jax.ShapeDtypeStruct((n, LR_PAD), BF16),
        ],
        scratch_shapes=[pltpu.VMEM((2, tm, d), BF16)],
        compiler_params=_cparams(("arbitrary",)),
        name="in_proj",
    )(x2, x2, mod_l, mod_l, norm_w, q_scale, w_head, w_tail, w_lr)


NA_ROWS_PER_STEP = 8
NA_ROW_UNROLL = 4
NA_BAND = WIN_R * GRID_W
NA_VARIANTS = 8


def _na_kernel(q_ref, k_ref, v_ref, bias_ref, o_ref, *, rows_total):
    step = pl.program_id(1)
    lane = lax.broadcasted_iota(jnp.int32, (GRID_W, HEAD_LANES), 1)
    low = lane < NA_HEAD_DIM

    def row_group(gi, carry):
        items = []
        for u in range(NA_ROW_UNROLL):
            rr = gi * NA_ROW_UNROLL + u
            r = step * NA_ROWS_PER_STEP + rr
            row_start = jnp.clip(r - WIN_R // 2, 0, rows_total - WIN_R)
            variant = jnp.where(r < WIN_R // 2, r,
                                jnp.where(r > rows_total - WIN_R // 2, r - (rows_total - WIN_R), WIN_R // 2))
            k0 = pl.multiple_of(row_start * GRID_W, GRID_W)
            q0 = pl.multiple_of(rr * GRID_W, GRID_W)
            for hp in range(NA_HEADS // 2):
                items.append((q0, k0, variant, hp, slice(hp * HEAD_LANES, (hp + 1) * HEAD_LANES)))
        scores = []
        for q0, k0, variant, hp, hs in items:
            qp = q_ref[0, pl.ds(q0, GRID_W), hs]
            zero = jnp.zeros_like(qp)
            q2 = jnp.concatenate([jnp.where(low, qp, zero), jnp.where(low, zero, qp)], axis=0)
            scores.append(_dot_nt(q2, k_ref[0, pl.ds(k0, NA_BAND), hs]) + bias_ref[0, variant, hp])
        probs = []
        for s in scores:
            e = jnp.exp(s - jnp.max(s, axis=-1, keepdims=True))
            probs.append((e.astype(BF16), jnp.sum(e, axis=-1, keepdims=True)))
        for (q0, k0, variant, hp, hs), (e16, denom) in zip(items, probs):
            o2 = _dot(e16, v_ref[0, pl.ds(k0, NA_BAND), hs]) / denom
            o_ref[pl.ds(q0, GRID_W), hs] = jnp.where(low, o2[:GRID_W], o2[GRID_W:]).astype(BF16)
        return carry

    lax.fori_loop(0, NA_ROWS_PER_STEP // NA_ROW_UNROLL, row_group, 0)


def _na_bias_tables(rpb, rows):
    depth = rpb.shape[0]
    wr = min(WIN_R, rows)
    col = np.arange(GRID_W)
    col_start = np.clip(col - WIN_C // 2, 0, GRID_W - WIN_C)
    col_mask = (col[None, :] >= col_start[:, None]) & (col[None, :] < col_start[:, None] + WIN_C)
    col_off = np.clip(col[None, :] - col[:, None], -(WIN_C - 1), WIN_C - 1) + (WIN_C - 1)
    n_ro, n_co = 2 * WIN_R - 1, 2 * WIN_C - 1
    onehot = (col_off.reshape(-1)[None, :] == np.arange(n_co)[:, None]).astype(np.float32)
    by_col = jnp.dot(rpb.astype(F32).reshape(depth * NA_HEADS * n_ro, n_co), onehot, precision=lax.Precision.HIGHEST)
    by_col = by_col.reshape(depth, NA_HEADS, n_ro, GRID_W, GRID_W)
    by_col = jnp.where(col_mask[None, None, None], by_col, NEG_INF)
    rep_rows = list(range(WIN_R // 2)) + [WIN_R // 2] + [rows - WIN_R + v for v in range(WIN_R // 2 + 1, WIN_R)]
    tabs = []
    for r in rep_rows:
        row_start = int(np.clip(r - WIN_R // 2, 0, rows - wr))
        ro0 = row_start - r + (WIN_R - 1)
        tabs.append(by_col[:, :, ro0:ro0 + wr])
    tabs = jnp.stack(tabs, axis=1).transpose(0, 1, 2, 4, 3, 5)
    return tabs.reshape(depth, NA_VARIANTS, NA_HEADS // 2, 2 * GRID_W, wr * GRID_W)


def _neighborhood_attention(p3, bias_tabs, layer, bsz, t):
    rows = t // GRID_W
    assert rows >= 2 * WIN_R and rows % NA_ROWS_PER_STEP == 0
    n = bsz * t
    tq = NA_ROWS_PER_STEP * GRID_W
    steps = t // tq
    return pl.pallas_call(
        functools.partial(_na_kernel, rows_total=rows),
        grid=(bsz, steps),
        in_specs=[
            pl.BlockSpec((1, tq, PCHUNK), lambda b, i: (CH_NA_Q, b * steps + i, 0)),
            pl.BlockSpec((1, t, PCHUNK), lambda b, i: (CH_NA_K, b, 0)),
            pl.BlockSpec((1, t, PCHUNK), lambda b, i: (CH_NA_V, b, 0)),
            _layer_resident(bias_tabs.shape, layer),
        ],
        out_specs=pl.BlockSpec((tq, NA_WIDTH), lambda b, i: (b * steps + i, 0)),
        out_shape=jax.ShapeDtypeStruct((n, NA_WIDTH), BF16),
        compiler_params=_cparams(("arbitrary", "arbitrary")),
        name="neighborhood_attn",
    )(p3, p3, p3, bias_tabs)


def _rec_kernel(*refs, mode, reverse, final, chunk, block):
    it = iter(refs)
    q_ref = next(it)
    if mode == "gla":
        v_ref, lr_ref, uph_ref, upl_ref, gbias_ref = (next(it) for _ in range(5))
    else:
        f_ref, v_ref, lb_ref = (next(it) for _ in range(3))
    tri_ref = next(it)
    if final:
        prev_ref, gate_ref, nw_ref = (next(it) for _ in range(3))
    o_ref = next(it)
    st_ref, st_old, kbuf, wbuf, qbuf = (next(it) for _ in range(5))

    c, s = chunk, block
    nb = c // s
    dk = GLA_DK if mode == "gla" else HEAD_LANES
    width = N_REC_HEADS * dk
    n_chunks = REC_TOKENS_PER_STEP // c

    def key_lanes(h):
        if dk == HEAD_LANES:
            return slice(h * HEAD_LANES, (h + 1) * HEAD_LANES), None
        t = (h * dk) // HEAD_LANES
        lane = lax.broadcasted_iota(jnp.int32, (1, HEAD_LANES), 1)
        mine = (lane // dk) == (h - t * (HEAD_LANES // dk))
        return slice(t * HEAD_LANES, (t + 1) * HEAD_LANES), mine

    def own(x, mine):
        return x if mine is None else jnp.where(mine, x, jnp.zeros_like(x))

    @pl.when(pl.program_id(1) == 0)
    def _():
        st_ref[...] = jnp.zeros_like(st_ref)

    ii = lax.broadcasted_iota(jnp.int32, (c, c), 0)
    jj = lax.broadcasted_iota(jnp.int32, (c, c), 1)
    bi, bj = ii // s, jj // s
    gap = (bj - bi) if reverse else (bi - bj)
    seen = (jj >= ii) if reverse else (jj <= ii)
    cls = jnp.where(gap == 0, jnp.where(seen, 0, -1), jnp.where(gap > 0, gap, -1))
    row_id = lax.broadcasted_iota(jnp.int32, (c, 1), 0)
    col_id = lax.broadcasted_iota(jnp.int32, (1, c), 1)

    def prev_block(i, steps=1):
        return i + steps if reverse else i - steps

    def per_block(vals):
        return jnp.concatenate([jnp.broadcast_to(x, (s, width)) for x in vals], axis=0)

    def chunk_inputs(rows):
        if mode == "gla":
            q = q_ref[0, rows, :width].astype(F32)
            k = q_ref[0, rows, width:].astype(F32)
            lr = lr_ref[rows, :]
            z = _dot(lr, uph_ref[...]) + _dot(lr, upl_ref[...]) + gbias_ref[...]
            g = (jnp.minimum(z, 0.0) - jnp.log(1.0 + jnp.exp(-jnp.abs(z)))) * (1.0 / GLA_TAU)
        else:
            z = f_ref[0, rows, :].astype(F32)
            sig, sig_neg = _sigmoid_pair(z)
            lb = lb_ref[...]
            g = jnp.log(jnp.maximum(lb + (1.0 - lb) * sig, F_FLOOR))
            k = (1.0 - lb) * sig_neg
            q = _silu(q_ref[0, rows, :].astype(F32))
        return q, k, g

    def chunk_factors(q, k, w, exact):
        w3 = w.reshape(nb, s, width)
        last = 0 if reverse else s - 1
        wb = [w3[i, last:last + 1, :] for i in range(nb)]
        order = list(range(nb - 1, -1, -1)) if reverse else list(range(nb))
        beta = [None] * nb
        run = jnp.zeros((1, width), F32)
        for i in order:
            beta[i] = run
            run = run + wb[i]
        total = run
        ewb = [jnp.exp(x) for x in wb]
        qd = q * jnp.exp(w)
        ks = k * jnp.exp(per_block(wb) - w)
        kd16 = None if exact else (k * jnp.exp(-w)).astype(BF16)
        one = jnp.ones((1, width), F32)
        lhs, fac = [qd.astype(BF16)], [one] * nb
        for dgap in range(2, nb):
            fac = [fac[i] * (ewb[prev_block(i, dgap - 1)] if 0 <= prev_block(i, dgap - 1) < nb else one)
                   for i in range(nb)]
            lhs.append((qd * per_block(fac)).astype(BF16))
        ks16 = ks.astype(BF16)
        if nb == 1:
            return lhs, kd16, ks16, lhs[0], ks16, jnp.exp(total)
        qe16 = (qd * jnp.exp(per_block(beta))).astype(BF16)
        ke16 = (ks * jnp.exp(per_block([total - beta[i] - wb[i] for i in range(nb)]))).astype(BF16)
        return lhs, kd16, ks16, qe16, ke16, jnp.exp(total)

    def head_scores(h, lhs, kd16, ks16, sd):
        ks_, mine = key_lanes(h)
        if sd is None:
            sd = _dot_nt(own(lhs[0][:, ks_], mine), kd16[:, ks_])
        a = jnp.where(cls == 0, sd, 0.0)
        if nb > 1:
            def far_rows(dgap):
                return slice(0, c - dgap * s) if reverse else slice(dgap * s, c)

            parts = [own(lhs[max(dgap - 1, 0)][:, ks_], mine)[far_rows(dgap)] for dgap in range(1, nb)]
            so = _dot_nt(jnp.concatenate(parts, axis=0), ks16[:, ks_])
            start = 0
            for dgap in range(1, nb):
                n_far = c - dgap * s
                blank = jnp.zeros((dgap * s, c), F32)
                part = so[start:start + n_far]
                start += n_far
                full = jnp.concatenate([part, blank] if reverse else [blank, part], axis=0)
                a = jnp.where(cls == dgap, full, a)
        return a.astype(BF16)

    def chunk_output(h, a16, qe16, v16, st):
        ks_, mine = key_lanes(h)
        lhs = jnp.concatenate([own(qe16[:, ks_], mine), a16], axis=1)
        return _dot(lhs, jnp.concatenate([st.astype(BF16), v16], axis=0))

    def emit(h, rows, o):
        hs = slice(h * HEAD_LANES, (h + 1) * HEAD_LANES)
        if final:
            tot = prev_ref[rows, hs] + o
            ms = jnp.mean(tot * tot, axis=-1, keepdims=True)
            y = tot * lax.rsqrt(ms + RMS_EPS) * nw_ref[...]
            o_ref[rows, hs] = (y * _silu(gate_ref[0, rows, hs].astype(F32))).astype(o_ref.dtype)
        else:
            o_ref[rows, hs] = o

    def exact_same_block_scores(u, h):
        hs, mine = key_lanes(h)

        def col_body(jg, acc):
            j0 = pl.multiple_of(jg * SUBLANES, SUBLANES)
            k8 = kbuf[u, pl.ds(j0, SUBLANES), hs]
            w8 = wbuf[u, pl.ds(j0, SUBLANES), hs]
            for jr in range(SUBLANES):
                j = j0 + jr
                same = (row_id // s) == (j // s)
                valid = same & ((row_id <= j) if reverse else (row_id >= j))
                e = jnp.exp(jnp.where(valid, wbuf[u, :, hs] - w8[jr:jr + 1], 0.0))
                t = jnp.sum(own(qbuf[u, :, hs], mine) * k8[jr:jr + 1] * e, axis=-1, keepdims=True)
                acc = jnp.where(valid & (col_id == j), t, acc)
            return acc

        return lax.fori_loop(0, c // SUBLANES, col_body, jnp.zeros((c, c), F32))

    def group_body(gi, carry):
        all_rows = []
        for u in range(REC_UNROLL):
            ci = gi * REC_UNROLL + u
            cc = (n_chunks - 1 - ci) if reverse else ci
            all_rows.append(pl.ds(pl.multiple_of(cc * c, c), c))
        ws, facs = [], []
        for u, rows in enumerate(all_rows):
            q, k, g = chunk_inputs(rows)
            g_hi, g_lo = _split_bf16(g)
            w2 = _dot(tri_ref[...], jnp.concatenate([g_hi, g_lo], axis=1))
            w = w2[:, :width] + w2[:, width:]
            ws.append(w)
            facs.append(chunk_factors(q, k, w, exact=False))
            qbuf[u] = q
            kbuf[u] = k
            wbuf[u] = w
        a16 = [[head_scores(h, f[0], f[1], f[2], None) for h in range(N_REC_HEADS)] for f in facs]
        vs, incs = [], []
        for u, rows in enumerate(all_rows):
            ke16 = facs[u][4]
            vs.append([v_ref[0, rows, h * HEAD_LANES:(h + 1) * HEAD_LANES] for h in range(N_REC_HEADS)])
            incs.append([_dot_tn(own(ke16[:, key_lanes(h)[0]], key_lanes(h)[1]), vs[u][h])
                         for h in range(N_REC_HEADS)])
        states = []
        for h in range(N_REC_HEADS):
            ks_, _ = key_lanes(h)
            st = st_ref[h]
            states.append([])
            for u in range(REC_UNROLL):
                states[h].append(st)
                st_old[u, h] = st
                decay_col = jnp.broadcast_to(facs[u][5][:, ks_], (HEAD_LANES, HEAD_LANES)).T
                st = st * decay_col + incs[u][h]
            st_ref[h] = st
        for u, rows in enumerate(all_rows):
            for h in range(N_REC_HEADS):
                emit(h, rows, chunk_output(h, a16[u][h], facs[u][3], vs[u][h], states[h][u]))

        w_min = functools.reduce(jnp.minimum, ws)

        @pl.when(jnp.min(w_min) < -DECAY_SPAN_LIMIT)
        def _():
            for u, rows in enumerate(all_rows):
                lhs, _, ks16, qe16, _, _ = chunk_factors(qbuf[u], kbuf[u], wbuf[u], exact=True)
                for h in range(N_REC_HEADS):
                    hs = slice(h * HEAD_LANES, (h + 1) * HEAD_LANES)
                    a = head_scores(h, lhs, None, ks16, exact_same_block_scores(u, h))
                    emit(h, rows, chunk_output(h, a, qe16, v_ref[0, rows, hs], st_old[u, h]))

        return carry

    lax.fori_loop(0, n_chunks // REC_UNROLL, group_body, 0)


def _scan_mask(chunk, block, reverse):
    i = np.arange(chunk)
    same = (i[:, None] // block) == (i[None, :] // block)
    seen = (i[None, :] >= i[:, None]) if reverse else (i[None, :] <= i[:, None])
    return jnp.asarray((same & seen).astype(np.float32), BF16)


def _recurrent_pass(mode, reverse, final, chunk, block, p3, bsz, t, extra, prev=None, norm_w=None):
    n = bsz * t
    tb = REC_TOKENS_PER_STEP
    nblk = t // tb
    width = N_REC_HEADS * HEAD_LANES
    key_width = N_REC_HEADS * (GLA_DK if mode == "gla" else HEAD_LANES)

    def tok(b, i):
        return b * nblk + ((nblk - 1 - i) if reverse else i)

    def chunk_spec(ch):
        return pl.BlockSpec((1, tb, PCHUNK), lambda b, i: (ch, tok(b, i), 0))

    tok_spec = lambda w: pl.BlockSpec((tb, w), lambda b, i: (tok(b, i), 0))
    operands, specs = [], []
    if mode == "gla":
        lr, up_hi, up_lo, gbias = extra
        operands += [p3, p3, lr, up_hi, up_lo, gbias]
        specs += [chunk_spec(CH_GLA_QK), chunk_spec(CH_GLA_V), tok_spec(LR_PAD),
                  _resident(up_hi.shape), _resident(up_lo.shape), _resident(gbias.shape)]
        gate_ch = CH_GLA_G
    else:
        (lb,) = extra
        operands += [p3, p3, p3, lb]
        specs += [chunk_spec(CH_HG_Q), chunk_spec(CH_HG_FB if reverse else CH_HG_FF), chunk_spec(CH_HG_I),
                  _resident(lb.shape)]
        gate_ch = CH_HG_G
    tri = _scan_mask(chunk, block, reverse)
    operands.append(tri)
    specs.append(_resident(tri.shape))
    if final:
        operands += [prev, p3, norm_w]
        specs += [tok_spec(width), chunk_spec(gate_ch), _resident(norm_w.shape)]
    out_dtype = BF16 if final else F32
    return pl.pallas_call(
        functools.partial(_rec_kernel, mode=mode, reverse=reverse, final=final, chunk=chunk, block=block),
        grid=(bsz, nblk),
        in_specs=specs,
        out_specs=tok_spec(width),
        out_shape=jax.ShapeDtypeStruct((n, width), out_dtype),
        scratch_shapes=[
            pltpu.VMEM((N_REC_HEADS, HEAD_LANES, HEAD_LANES), F32),
            pltpu.VMEM((REC_UNROLL, N_REC_HEADS, HEAD_LANES, HEAD_LANES), F32),
            pltpu.VMEM((REC_UNROLL, chunk, key_width), F32),
            pltpu.VMEM((REC_UNROLL, chunk, key_width), F32),
            pltpu.VMEM((REC_UNROLL, chunk, key_width), F32),
        ],
        compiler_params=_cparams(("arbitrary", "arbitrary")),
        name=f"{mode}_{'bwd' if reverse else 'fwd'}",
    )(*operands)


GLA_CHUNK, GLA_BLOCK = 128, 128
HGRN_CHUNK, HGRN_BLOCK = 64, 16


def _gla_branch(p3, lr, up_pads, gbias_pads, norm_w, bsz, t):
    fwd = _recurrent_pass("gla", False, False, GLA_CHUNK, GLA_BLOCK, p3, bsz, t,
                          (lr, up_pads[0][0], up_pads[0][1], gbias_pads[0]))
    return _recurrent_pass("gla", True, True, GLA_CHUNK, GLA_BLOCK, p3, bsz, t,
                           (lr, up_pads[1][0], up_pads[1][1], gbias_pads[1]), prev=fwd, norm_w=norm_w)


def _hgrn_branch(p3, lbs, norm_w, bsz, t):
    fwd = _recurrent_pass("hgrn", False, False, HGRN_CHUNK, HGRN_BLOCK, p3, bsz, t, (lbs[0],))
    return _recurrent_pass("hgrn", True, True, HGRN_CHUNK, HGRN_BLOCK, p3, bsz, t, (lbs[1],),
                           prev=fwd, norm_w=norm_w)


MLP_ROWS = 512
MLP_GROUP_ROWS = 256


def _mlp_kernel(x_ref, ona_ref, ogla_ref, ohg_ref, gates_ref, mod_ref, wna_ref, wgla_ref, whg_ref, wout_ref,
                n2_ref, wg_ref, wu_ref, wd_ref, fn_ref, o_ref, *, last):
    d = D_MODEL
    gate1 = mod_ref[0, :, 2 * d:3 * d]
    shift2 = mod_ref[0, :, 3 * d:4 * d]
    scale2 = mod_ref[0, :, 4 * d:5 * d]
    gate2 = mod_ref[0, :, 5 * d:6 * d]
    groups = [slice(i * MLP_GROUP_ROWS, (i + 1) * MLP_GROUP_ROWS) for i in range(MLP_ROWS // MLP_GROUP_ROWS)]

    def gate(rs, i):
        g = jnp.concatenate([gates_ref[2 * i, rs, :].astype(F32), gates_ref[2 * i + 1, rs, :].astype(F32)], axis=1)
        return _sigmoid(g)

    merged = [(gate(rs, 0) * _dot(ona_ref[rs, :], wna_ref[0])
               + gate(rs, 1) * _dot(ogla_ref[rs, :], wgla_ref[0])
               + gate(rs, 2) * _dot(ohg_ref[rs, :], whg_ref[0])).astype(BF16) for rs in groups]
    x1 = [x_ref[rs, :] + gate1 * _dot(m, wout_ref[0]) for rs, m in zip(groups, merged)]
    h = [_rmsnorm_mod(x, n2_ref[0], scale2, shift2).astype(BF16) for x in x1]
    act = [(_silu(_dot(hh, wg_ref[0])) * _dot(hh, wu_ref[0])).astype(BF16) for hh in h]
    for rs, x, a in zip(groups, x1, act):
        x2 = x + gate2 * _dot(a, wd_ref[0])
        if last:
            ms = jnp.mean(x2 * x2, axis=-1, keepdims=True)
            x2 = x2 * lax.rsqrt(ms + RMS_EPS) * fn_ref[...]
        o_ref[rs, :] = x2


def _merge_mlp(x2, o_na, o_gla, o_hg, p3, mod_l, layer_wts, final_norm_w, layer, tokens_per_batch, last):
    n, d = x2.shape
    tm = MLP_ROWS
    per_batch = tokens_per_batch // tm
    row = lambda w: pl.BlockSpec((tm, w), lambda i: (i, 0))
    return pl.pallas_call(
        functools.partial(_mlp_kernel, last=last),
        grid=(n // tm,),
        in_specs=[
            row(d), row(NA_WIDTH), row(NA_WIDTH), row(NA_WIDTH),
            pl.BlockSpec((N_GATE_CHUNKS, tm, PCHUNK), lambda i: (CH_GATES // N_GATE_CHUNKS, i, 0)),
            pl.BlockSpec((1, 1, N_MOD * d), lambda i: (i // per_batch, 0, 0)),
        ] + [_layer_resident(w.shape, layer) for w in layer_wts] + [_resident(final_norm_w.shape)],
        out_specs=row(d),
        out_shape=jax.ShapeDtypeStruct((n, d), F32),
        compiler_params=_cparams(("arbitrary",)),
        name="merge_mlp",
    )(x2, o_na, o_gla, o_hg, p3, mod_l, *layer_wts, final_norm_w)


def _split_w_in(w):
    head = N_HEAD_CHUNKS * PCHUNK
    tail = head + 2 * GLA_RANK
    assert w.shape[2] - tail == N_TAIL_CHUNKS * PCHUNK
    w_t = jnp.swapaxes(w, 1, 2)
    w_lr = jnp.pad(w_t[:, head:tail], ((0, 0), (0, LR_PAD - 2 * GLA_RANK), (0, 0)))
    return w_t[:, :head].astype(BF16), w_t[:, tail:].astype(BF16), w_lr.astype(BF16)


def _query_scales():
    qs = np.ones((N_HEAD_CHUNKS, 1, PCHUNK), np.float32)
    qs[CH_NA_Q] = NA_HEAD_DIM ** -0.5
    qs[CH_GLA_QK, :, :GLA_HEADS * GLA_DK] = GLA_DK ** -0.5
    return jnp.asarray(qs)


def kernel(x, c, w_ada, b_ada, norm1_w, w_in, na_rpb, gla_lr_up, gla_lr_bias, gla_norm_w, hgrn_lb_logits, hgrn_norm_w, w_proj_na, w_proj_gla, w_proj_hgrn, w_out, norm2_w, w_ffn_gate, w_ffn_up, w_ffn_down, final_norm_w):
    bsz, t, d = x.shape
    depth = w_in.shape[0]
    assert d == D_MODEL and t % REC_TOKENS_PER_STEP == 0 and t % INPROJ_ROWS == 0
    n = bsz * t

    mods = _modulation(c, w_ada, b_ada).reshape(depth, bsz, 1, N_MOD * d)
    lb_p = jax.nn.softmax(hgrn_lb_logits.astype(F32), axis=0)
    lb_all = jnp.clip(jnp.cumsum(lb_p, axis=0) - lb_p[0], 0.0, 1.0)

    w_head, w_tail, w_lr = _split_w_in(w_in)
    q_scale = _query_scales()
    bias_tabs = _na_bias_tables(na_rpb, t // GRID_W)
    norm1 = norm1_w.reshape(depth, 1, d)
    mlp_wts = [w_proj_na.astype(BF16), w_proj_gla.astype(BF16), w_proj_hgrn.astype(BF16), w_out.astype(BF16),
               norm2_w.reshape(depth, 1, d), w_ffn_gate.astype(BF16), w_ffn_up.astype(BF16),
               w_ffn_down.astype(BF16)]
    up = jnp.zeros((depth, 2, LR_PAD, GLA_HEADS * GLA_DK), F32)
    for s in range(2):
        up = up.at[:, s, s * GLA_RANK:(s + 1) * GLA_RANK].set(gla_lr_up[:, s])
    up_hi = up.astype(BF16)
    up_lo = (up - up_hi.astype(F32)).astype(BF16)

    xf = x.reshape(n, d)
    for l in range(depth):
        p3, lr = _in_projection(xf, mods[l], norm1, q_scale, w_head, w_tail, w_lr, l, t)

        o_na = _neighborhood_attention(p3, bias_tabs, l, bsz, t)

        up_pads = [(up_hi[l, s], up_lo[l, s]) for s in range(2)]
        gbias = [gla_lr_bias[l, s].reshape(1, -1) for s in range(2)]
        o_gla = _gla_branch(p3, lr, up_pads, gbias, gla_norm_w[l].reshape(1, -1), bsz, t)

        lbs = [lb_all[l, s * HGRN_KEY_WIDTH:(s + 1) * HGRN_KEY_WIDTH].reshape(1, -1) for s in range(2)]
        o_hg = _hgrn_branch(p3, lbs, hgrn_norm_w[l].reshape(1, -1), bsz, t)

        xf = _merge_mlp(xf, o_na, o_gla, o_hg, p3, mods[l], mlp_wts, final_norm_w.reshape(1, d), l, t,
                        last=(l == depth - 1))
    return xf.reshape(bsz, t, d)
```

```python
import functools

import jax
import jax.numpy as jnp
import numpy as np
from jax import lax
from jax.experimental import pallas as pl
from jax.experimental.pallas import tpu as pltpu

F32 = jnp.float32
BF16 = jnp.bfloat16

D_MODEL = 1024
GRID_W = 64
WIN_R = 8
WIN_C = 16
NA_HEADS = 8
NA_HEAD_DIM = 64
NA_WIDTH = 512
GLA_HEADS = 4
GLA_DK = 64
GLA_RANK = 16
GLA_TAU = 16.0
HGRN_KEY_WIDTH = 512
N_MOD = 6
FFN_HIDDEN = 2816
RMS_EPS = 1e-6
NEG_INF = -1e30
F_FLOOR = 1e-30

PCHUNK = 512
N_GATE_CHUNKS = 6
N_HEAD_CHUNKS = 6
N_TAIL_CHUNKS = 11
CH_NA_Q, CH_NA_K, CH_NA_V, CH_GLA_QK, CH_GLA_V, CH_GLA_G = range(N_HEAD_CHUNKS)
CH_GATES = N_HEAD_CHUNKS
CH_HG_Q, CH_HG_FF, CH_HG_FB, CH_HG_I, CH_HG_G = range(CH_GATES + N_GATE_CHUNKS, CH_GATES + N_GATE_CHUNKS + 5)
N_PCHUNKS = N_HEAD_CHUNKS + N_TAIL_CHUNKS
LR_PAD = 128
HEAD_LANES = 128
SUBLANES = 8
N_REC_HEADS = 4

VMEM_LIMIT_BYTES = 56 * 1024 * 1024

REC_TOKENS_PER_STEP = 512
REC_MAX_UNROLL = 8
DECAY_SPAN_LIMIT = 60.0


def _cparams(sem):
    return pltpu.CompilerParams(dimension_semantics=sem, vmem_limit_bytes=VMEM_LIMIT_BYTES)


def _resident(shape):
    nd = len(shape)
    return pl.BlockSpec(shape, lambda *_: (0,) * nd, pipeline_mode=pl.Buffered(1))


def _layer_resident(shape, layer):
    nd = len(shape)
    return pl.BlockSpec((1,) + tuple(shape[1:]), lambda *_: (layer,) + (0,) * (nd - 1), pipeline_mode=pl.Buffered(1))


def _dot(a, b):
    return jnp.dot(a, b, preferred_element_type=F32)


def _dot_nt(a, b):
    return lax.dot_general(a, b, (((1,), (1,)), ((), ())), preferred_element_type=F32)


def _dot_tn(a, b):
    return lax.dot_general(a, b, (((0,), (0,)), ((), ())), preferred_element_type=F32)


def _sigmoid_pair(z):
    e = jnp.exp(-jnp.abs(z))
    r = 1.0 / (1.0 + e)
    er = e * r
    pos = z >= 0
    return jnp.where(pos, r, er), jnp.where(pos, er, r)


def _sigmoid(z):
    return 0.5 * jnp.tanh(0.5 * z) + 0.5


def _silu(z):
    return z * _sigmoid(z)


def _split_bf16(a):
    hi = a.astype(BF16)
    lo = (a - hi.astype(F32)).astype(BF16)
    return hi, lo


MOD_COLS = 1536


def _mod_kernel(c_ref, w_ref, b_ref, o_ref):
    c_act = _silu(c_ref[...])
    c_hi, c_lo = _split_bf16(c_act)
    w_hi, w_lo = _split_bf16(w_ref[0])
    acc = _dot(c_hi, w_hi) + _dot(c_lo, w_hi) + _dot(c_hi, w_lo)
    o_ref[0] = acc + b_ref[0]


def _modulation(c, w_ada, b_ada):
    depth, d, width = w_ada.shape
    bsz = c.shape[0]
    rows = 8
    c_pad = jnp.zeros((rows, d), F32).at[:bsz].set(c)
    out = pl.pallas_call(
        _mod_kernel,
        grid=(depth, width // MOD_COLS),
        in_specs=[
            pl.BlockSpec((rows, d), lambda l, j: (0, 0)),
            pl.BlockSpec((1, d, MOD_COLS), lambda l, j: (l, 0, j)),
            pl.BlockSpec((1, 1, MOD_COLS), lambda l, j: (l, 0, j)),
        ],
        out_specs=pl.BlockSpec((1, rows, MOD_COLS), lambda l, j: (l, 0, j)),
        out_shape=jax.ShapeDtypeStruct((depth, rows, width), F32),
        compiler_params=_cparams(("arbitrary", "arbitrary")),
        name="adaln_mod",
    )(c_pad, w_ada, b_ada.reshape(depth, 1, width))
    return out[:, :bsz]


INPROJ_ROWS = 512
INPROJ_NORM_AFTER = (1, 3)


def _rmsnorm_mod(x, w, scale, shift):
    ms = jnp.mean(x * x, axis=-1, keepdims=True)
    y = x * lax.rsqrt(ms + RMS_EPS) * w
    return y * (1.0 + scale) + shift


def _inproj_kernel(x_ref, xn_ref, mod_ref, modn_ref, nw_ref, qs_ref, w_ref, p_ref, lr_ref, h_ref):
    d = D_MODEL
    step = pl.program_id(0)
    cur = step % 2
    lr0 = N_HEAD_CHUNKS * PCHUNK
    tail0 = lr0 + 2 * GLA_RANK

    def normed(xr, mr, rows):
        return _rmsnorm_mod(xr[rows, :], nw_ref[0], mr[0, :, d:2 * d], mr[0, :, 0:d]).astype(BF16)

    @pl.when(step == 0)
    def _():
        h_ref[0] = normed(x_ref, mod_ref, slice(None))

    half = INPROJ_ROWS // 2
    for j in range(N_PCHUNKS):
        if j < N_HEAD_CHUNKS:
            p_ref[j] = (_dot_nt(h_ref[cur], w_ref[0, j * PCHUNK:(j + 1) * PCHUNK, :]) * qs_ref[j]).astype(BF16)
        else:
            t = j - N_HEAD_CHUNKS
            dst = CH_HG_Q + t if t < N_TAIL_CHUNKS - N_GATE_CHUNKS else CH_GATES + t - (N_TAIL_CHUNKS - N_GATE_CHUNKS)
            p_ref[dst] = _dot_nt(h_ref[cur], w_ref[0, tail0 + t * PCHUNK:tail0 + (t + 1) * PCHUNK, :]).astype(BF16)
        if j in INPROJ_NORM_AFTER:
            rows = slice(0, half) if j == INPROJ_NORM_AFTER[0] else slice(half, INPROJ_ROWS)
            h_ref[1 - cur, rows, :] = normed(xn_ref, modn_ref, rows)
    lr = _dot_nt(h_ref[cur], w_ref[0, lr0:tail0, :])
    lr_ref[...] = jnp.concatenate([lr, jnp.zeros((INPROJ_ROWS, LR_PAD - 2 * GLA_RANK), F32)], axis=1).astype(BF16)


def _in_projection(x2, mod_l, norm_w, q_scale, w_t, layer, tokens_per_batch):
    n, d = x2.shape
    tm = INPROJ_ROWS
    per_batch = tokens_per_batch // tm
    last = n // tm - 1
    nxt = lambda i: jnp.minimum(i + 1, last)
    return pl.pallas_call(
        _inproj_kernel,
        grid=(n // tm,),
        in_specs=[
            pl.BlockSpec((tm, d), lambda i: (i, 0)),
            pl.BlockSpec((tm, d), lambda i: (nxt(i), 0)),
            pl.BlockSpec((1, 1, N_MOD * d), lambda i: (i // per_batch, 0, 0)),
            pl.BlockSpec((1, 1, N_MOD * d), lambda i: (nxt(i) // per_batch, 0, 0)),
            _layer_resident(norm_w.shape, layer),
            _resident(q_scale.shape),
            _layer_resident(w_t.shape, layer),
        ],
        out_specs=[
            pl.BlockSpec((N_PCHUNKS, tm, PCHUNK), lambda i: (0, i, 0)),
            pl.BlockSpec((tm, LR_PAD), lambda i: (i, 0)),
        ],
        out_shape=[
            jax.ShapeDtypeStruct((N_PCHUNKS, n, PCHUNK), BF16),
            jax.ShapeDtypeStruct((n, LR_PAD), BF16),
        ],
        scratch_shapes=[pltpu.VMEM((2, tm, d), BF16)],
        compiler_params=_cparams(("arbitrary",)),
        name="in_proj",
    )(x2, x2, mod_l, mod_l, norm_w, q_scale, w_t)


NA_ROWS_PER_STEP = 8
NA_ROW_UNROLL = 4
NA_BAND = WIN_R * GRID_W
NA_VARIANTS = 8


def _na_kernel(q_ref, k_ref, v_ref, bias_ref, o_ref, *, rows_total):
    step = pl.program_id(1)
    lane = lax.broadcasted_iota(jnp.int32, (GRID_W, HEAD_LANES), 1)
    low = lane < NA_HEAD_DIM

    def row_group(gi, carry):
        items = []
        for u in range(NA_ROW_UNROLL):
            rr = gi * NA_ROW_UNROLL + u
            r = step * NA_ROWS_PER_STEP + rr
            row_start = jnp.clip(r - WIN_R // 2, 0, rows_total - WIN_R)
            variant = jnp.where(r < WIN_R // 2, r,
                                jnp.where(r > rows_total - WIN_R // 2, r - (rows_total - WIN_R), WIN_R // 2))
            k0 = pl.multiple_of(row_start * GRID_W, GRID_W)
            q0 = pl.multiple_of(rr * GRID_W, GRID_W)
            for hp in range(NA_HEADS // 2):
                items.append((q0, k0, variant, hp, slice(hp * HEAD_LANES, (hp + 1) * HEAD_LANES)))
        scores = []
        for q0, k0, variant, hp, hs in items:
            qp = q_ref[0, pl.ds(q0, GRID_W), hs]
            zero = jnp.zeros_like(qp)
            q2 = jnp.concatenate([jnp.where(low, qp, zero), jnp.where(low, zero, qp)], axis=0)
            scores.append(_dot_nt(q2, k_ref[0, pl.ds(k0, NA_BAND), hs]) + bias_ref[0, variant, hp])
        probs = []
        for s in scores:
            e = jnp.exp(s - jnp.max(s, axis=-1, keepdims=True))
            probs.append((e.astype(BF16), jnp.sum(e, axis=-1, keepdims=True)))
        for (q0, k0, variant, hp, hs), (e16, denom) in zip(items, probs):
            o2 = _dot(e16, v_ref[0, pl.ds(k0, NA_BAND), hs]) / denom
            o_ref[pl.ds(q0, GRID_W), hs] = jnp.where(low, o2[:GRID_W], o2[GRID_W:]).astype(BF16)
        return carry

    lax.fori_loop(0, NA_ROWS_PER_STEP // NA_ROW_UNROLL, row_group, 0)


def _na_bias_tables(rpb, rows):
    depth = rpb.shape[0]
    wr = min(WIN_R, rows)
    col = np.arange(GRID_W)
    col_start = np.clip(col - WIN_C // 2, 0, GRID_W - WIN_C)
    col_mask = (col[None, :] >= col_start[:, None]) & (col[None, :] < col_start[:, None] + WIN_C)
    col_off = np.clip(col[None, :] - col[:, None], -(WIN_C - 1), WIN_C - 1) + (WIN_C - 1)
    n_ro, n_co = 2 * WIN_R - 1, 2 * WIN_C - 1
    onehot = (col_off.reshape(-1)[None, :] == np.arange(n_co)[:, None]).astype(np.float32)
    by_col = jnp.dot(rpb.astype(F32).reshape(depth * NA_HEADS * n_ro, n_co), onehot, precision=lax.Precision.HIGHEST)
    by_col = by_col.reshape(depth, NA_HEADS, n_ro, GRID_W, GRID_W)
    by_col = jnp.where(col_mask[None, None, None], by_col, NEG_INF)
    rep_rows = list(range(WIN_R // 2)) + [WIN_R // 2] + [rows - WIN_R + v for v in range(WIN_R // 2 + 1, WIN_R)]
    tabs = []
    for r in rep_rows:
        row_start = int(np.clip(r - WIN_R // 2, 0, rows - wr))
        ro0 = row_start - r + (WIN_R - 1)
        tabs.append(by_col[:, :, ro0:ro0 + wr])
    tabs = jnp.stack(tabs, axis=1).transpose(0, 1, 2, 4, 3, 5)
    return tabs.reshape(depth, NA_VARIANTS, NA_HEADS // 2, 2 * GRID_W, wr * GRID_W)


def _neighborhood_attention(p3, bias_tabs, layer, bsz, t):
    rows = t // GRID_W
    assert rows >= 2 * WIN_R and rows % NA_ROWS_PER_STEP == 0
    n = bsz * t
    tq = NA_ROWS_PER_STEP * GRID_W
    steps = t // tq
    return pl.pallas_call(
        functools.partial(_na_kernel, rows_total=rows),
        grid=(bsz, steps),
        in_specs=[
            pl.BlockSpec((1, tq, PCHUNK), lambda b, i: (CH_NA_Q, b * steps + i, 0)),
            pl.BlockSpec((1, t, PCHUNK), lambda b, i: (CH_NA_K, b, 0)),
            pl.BlockSpec((1, t, PCHUNK), lambda b, i: (CH_NA_V, b, 0)),
            _layer_resident(bias_tabs.shape, layer),
        ],
        out_specs=pl.BlockSpec((tq, NA_WIDTH), lambda b, i: (b * steps + i, 0)),
        out_shape=jax.ShapeDtypeStruct((n, NA_WIDTH), BF16),
        compiler_params=_cparams(("arbitrary", "arbitrary")),
        name="neighborhood_attn",
    )(p3, p3, p3, bias_tabs)


def _rec_kernel(*refs, mode, reverse, final, chunk, block, unroll):
    it = iter(refs)
    q_ref = next(it)
    if mode == "gla":
        v_ref, lr_ref, uph_ref, upl_ref, gbias_ref = (next(it) for _ in range(5))
    else:
        f_ref, v_ref, lb_ref = (next(it) for _ in range(3))
    tri_ref = next(it)
    if final:
        prev_ref, gate_ref, nw_ref = (next(it) for _ in range(3))
    o_ref = next(it)
    st_ref, st_old, kbuf, wbuf, qbuf = (next(it) for _ in range(5))

    c, s = chunk, block
    nb = c // s
    dk = GLA_DK if mode == "gla" else HEAD_LANES
    width = N_REC_HEADS * dk
    n_chunks = REC_TOKENS_PER_STEP // c

    def key_lanes(h):
        if dk == HEAD_LANES:
            return slice(h * HEAD_LANES, (h + 1) * HEAD_LANES), None
        t = (h * dk) // HEAD_LANES
        lane = lax.broadcasted_iota(jnp.int32, (1, HEAD_LANES), 1)
        mine = (lane // dk) == (h - t * (HEAD_LANES // dk))
        return slice(t * HEAD_LANES, (t + 1) * HEAD_LANES), mine

    def own(x, mine):
        return x if mine is None else jnp.where(mine, x, jnp.zeros_like(x))

    @pl.when(pl.program_id(1) == 0)
    def _():
        st_ref[...] = jnp.zeros_like(st_ref)

    ii = lax.broadcasted_iota(jnp.int32, (c, c), 0)
    jj = lax.broadcasted_iota(jnp.int32, (c, c), 1)
    bi, bj = ii // s, jj // s
    gap = (bj - bi) if reverse else (bi - bj)
    seen = (jj >= ii) if reverse else (jj <= ii)
    cls = jnp.where(gap == 0, jnp.where(seen, 0, -1), jnp.where(gap > 0, gap, -1))
    row_id = lax.broadcasted_iota(jnp.int32, (c, 1), 0)
    col_id = lax.broadcasted_iota(jnp.int32, (1, c), 1)

    def prev_block(i, steps=1):
        return i + steps if reverse else i - steps

    def per_block(vals):
        return jnp.concatenate([jnp.broadcast_to(x, (s, width)) for x in vals], axis=0)

    def chunk_inputs(rows):
        if mode == "gla":
            q = q_ref[0, rows, :width].astype(F32)
            k = q_ref[0, rows, width:].astype(F32)
            lr = lr_ref[rows, :]
            z = _dot(lr, uph_ref[...]) + _dot(lr, upl_ref[...]) + gbias_ref[...]
            g = (jnp.minimum(z, 0.0) - jnp.log(1.0 + jnp.exp(-jnp.abs(z)))) * (1.0 / GLA_TAU)
        else:
            z = f_ref[0, rows, :].astype(F32)
            sig, sig_neg = _sigmoid_pair(z)
            lb = lb_ref[...]
            g = jnp.log(jnp.maximum(lb + (1.0 - lb) * sig, F_FLOOR))
            k = (1.0 - lb) * sig_neg
            q = _silu(q_ref[0, rows, :].astype(F32))
        return q, k, g

    def chunk_factors(q, k, w, exact):
        w3 = w.reshape(nb, s, width)
        last = 0 if reverse else s - 1
        wb = [w3[i, last:last + 1, :] for i in range(nb)]
        order = list(range(nb - 1, -1, -1)) if reverse else list(range(nb))
        beta = [None] * nb
        run = jnp.zeros((1, width), F32)
        for i in order:
            beta[i] = run
            run = run + wb[i]
        total = run
        ewb = [jnp.exp(x) for x in wb]
        qd = q * jnp.exp(w)
        ks = k * jnp.exp(per_block(wb) - w)
        kd16 = None if exact else (ks * per_block([jnp.exp(-x) for x in wb])).astype(BF16)
        one = jnp.ones((1, width), F32)
        lhs, fac = [qd.astype(BF16)], [one] * nb
        for dgap in range(2, nb):
            fac = [fac[i] * (ewb[prev_block(i, dgap - 1)] if 0 <= prev_block(i, dgap - 1) < nb else one)
                   for i in range(nb)]
            lhs.append((qd * per_block(fac)).astype(BF16))
        ks16 = ks.astype(BF16)
        if nb == 1:
            return lhs, kd16, ks16, lhs[0], ks16, jnp.exp(total)
        qe16 = (qd * jnp.exp(per_block(beta))).astype(BF16)
        ke16 = (ks * jnp.exp(per_block([total - beta[i] - wb[i] for i in range(nb)]))).astype(BF16)
        return lhs, kd16, ks16, qe16, ke16, jnp.exp(total)

    def head_scores(h, lhs, kd16, ks16, sd):
        ks_, mine = key_lanes(h)
        if sd is None:
            sd = _dot_nt(own(lhs[0][:, ks_], mine), kd16[:, ks_])
        a = jnp.where(cls == 0, sd, 0.0)
        if nb > 1:
            def far_rows(dgap):
                return slice(0, c - dgap * s) if reverse else slice(dgap * s, c)

            parts = [own(lhs[max(dgap - 1, 0)][:, ks_], mine)[far_rows(dgap)] for dgap in range(1, nb)]
            so = _dot_nt(jnp.concatenate(parts, axis=0), ks16[:, ks_])
            start = 0
            for dgap in range(1, nb):
                n_far = c - dgap * s
                blank = jnp.zeros((dgap * s, c), F32)
                part = so[start:start + n_far]
                start += n_far
                full = jnp.concatenate([part, blank] if reverse else [blank, part], axis=0)
                a = jnp.where(cls == dgap, full, a)
        return a.astype(BF16)

    def chunk_output(h, a16, qe16, v16, st):
        ks_, mine = key_lanes(h)
        lhs = jnp.concatenate([own(qe16[:, ks_], mine), a16], axis=1)
        return _dot(lhs, jnp.concatenate([st.astype(BF16), v16], axis=0))

    def emit(h, rows, o):
        hs = slice(h * HEAD_LANES, (h + 1) * HEAD_LANES)
        if final:
            tot = prev_ref[rows, hs] + o
            ms = jnp.mean(tot * tot, axis=-1, keepdims=True)
            y = tot * lax.rsqrt(ms + RMS_EPS) * nw_ref[...]
            o_ref[rows, hs] = (y * _silu(gate_ref[0, rows, hs].astype(F32))).astype(o_ref.dtype)
        else:
            o_ref[rows, hs] = o

    def exact_same_block_scores(u, h):
        hs, mine = key_lanes(h)

        def col_body(jg, acc):
            j0 = pl.multiple_of(jg * SUBLANES, SUBLANES)
            k8 = kbuf[u, pl.ds(j0, SUBLANES), hs]
            w8 = wbuf[u, pl.ds(j0, SUBLANES), hs]
            for jr in range(SUBLANES):
                j = j0 + jr
                same = (row_id // s) == (j // s)
                valid = same & ((row_id <= j) if reverse else (row_id >= j))
                e = jnp.exp(jnp.where(valid, wbuf[u, :, hs] - w8[jr:jr + 1], 0.0))
                t = jnp.sum(own(qbuf[u, :, hs], mine) * k8[jr:jr + 1] * e, axis=-1, keepdims=True)
                acc = jnp.where(valid & (col_id == j), t, acc)
            return acc

        return lax.fori_loop(0, c // SUBLANES, col_body, jnp.zeros((c, c), F32))

    def group_body(gi, carry):
        all_rows = []
        for u in range(unroll):
            ci = gi * unroll + u
            cc = (n_chunks - 1 - ci) if reverse else ci
            all_rows.append(pl.ds(pl.multiple_of(cc * c, c), c))
        ws, facs = [], []
        for u, rows in enumerate(all_rows):
            q, k, g = chunk_inputs(rows)
            g_hi, g_lo = _split_bf16(g)
            w2 = _dot(tri_ref[...], jnp.concatenate([g_hi, g_lo], axis=1))
            w = w2[:, :width] + w2[:, width:]
            ws.append(w)
            facs.append(chunk_factors(q, k, w, exact=False))
            qbuf[u] = q
            kbuf[u] = k
            wbuf[u] = w
        a16 = [[head_scores(h, f[0], f[1], f[2], None) for h in range(N_REC_HEADS)] for f in facs]
        vs, incs = [], []
        for u, rows in enumerate(all_rows):
            ke16 = facs[u][4]
            vs.append([v_ref[0, rows, h * HEAD_LANES:(h + 1) * HEAD_LANES] for h in range(N_REC_HEADS)])
            incs.append([_dot_tn(own(ke16[:, key_lanes(h)[0]], key_lanes(h)[1]), vs[u][h])
                         for h in range(N_REC_HEADS)])
        states = []
        for h in range(N_REC_HEADS):
            ks_, _ = key_lanes(h)
            st = st_ref[h]
            states.append([])
            for u in range(unroll):
                states[h].append(st)
                st_old[u, h] = st
                decay_col = jnp.broadcast_to(facs[u][5][:, ks_], (HEAD_LANES, HEAD_LANES)).T
                st = st * decay_col + incs[u][h]
            st_ref[h] = st
        for u, rows in enumerate(all_rows):
            for h in range(N_REC_HEADS):
                emit(h, rows, chunk_output(h, a16[u][h], facs[u][3], vs[u][h], states[h][u]))

        w_min = functools.reduce(jnp.minimum, ws)

        @pl.when(jnp.min(w_min) < -DECAY_SPAN_LIMIT)
        def _():
            for u, rows in enumerate(all_rows):
                lhs, _, ks16, qe16, _, _ = chunk_factors(qbuf[u], kbuf[u], wbuf[u], exact=True)
                for h in range(N_REC_HEADS):
                    hs = slice(h * HEAD_LANES, (h + 1) * HEAD_LANES)
                    a = head_scores(h, lhs, None, ks16, exact_same_block_scores(u, h))
                    emit(h, rows, chunk_output(h, a, qe16, v_ref[0, rows, hs], st_old[u, h]))

        return carry

    lax.fori_loop(0, n_chunks // unroll, group_body, 0)


def _scan_mask(chunk, block, reverse):
    i = np.arange(chunk)
    same = (i[:, None] // block) == (i[None, :] // block)
    seen = (i[None, :] >= i[:, None]) if reverse else (i[None, :] <= i[:, None])
    return jnp.asarray((same & seen).astype(np.float32), BF16)


def _recurrent_pass(mode, reverse, final, chunk, block, p3, bsz, t, extra, prev=None, norm_w=None):
    n = bsz * t
    tb = REC_TOKENS_PER_STEP
    nblk = t // tb
    width = N_REC_HEADS * HEAD_LANES
    key_width = N_REC_HEADS * (GLA_DK if mode == "gla" else HEAD_LANES)
    unroll = min(tb // chunk, REC_MAX_UNROLL)

    def tok(b, i):
        return b * nblk + ((nblk - 1 - i) if reverse else i)

    def chunk_spec(ch):
        return pl.BlockSpec((1, tb, PCHUNK), lambda b, i: (ch, tok(b, i), 0))

    tok_spec = lambda w: pl.BlockSpec((tb, w), lambda b, i: (tok(b, i), 0))
    operands, specs = [], []
    if mode == "gla":
        lr, up_hi, up_lo, gbias = extra
        operands += [p3, p3, lr, up_hi, up_lo, gbias]
        specs += [chunk_spec(CH_GLA_QK), chunk_spec(CH_GLA_V), tok_spec(LR_PAD),
                  _resident(up_hi.shape), _resident(up_lo.shape), _resident(gbias.shape)]
        gate_ch = CH_GLA_G
    else:
        (lb,) = extra
        operands += [p3, p3, p3, lb]
        specs += [chunk_spec(CH_HG_Q), chunk_spec(CH_HG_FB if reverse else CH_HG_FF), chunk_spec(CH_HG_I),
                  _resident(lb.shape)]
        gate_ch = CH_HG_G
    tri = _scan_mask(chunk, block, reverse)
    operands.append(tri)
    specs.append(_resident(tri.shape))
    if final:
        operands += [prev, p3, norm_w]
        specs += [tok_spec(width), chunk_spec(gate_ch), _resident(norm_w.shape)]
    out_dtype = BF16 if final else F32
    return pl.pallas_call(
        functools.partial(_rec_kernel, mode=mode, reverse=reverse, final=final, chunk=chunk, block=block,
                          unroll=unroll),
        grid=(bsz, nblk),
        in_specs=specs,
        out_specs=tok_spec(width),
        out_shape=jax.ShapeDtypeStruct((n, width), out_dtype),
        scratch_shapes=[
            pltpu.VMEM((N_REC_HEADS, HEAD_LANES, HEAD_LANES), F32),
            pltpu.VMEM((unroll,N_REC_HEADS, HEAD_LANES, HEAD_LANES), F32),
            pltpu.VMEM((unroll,chunk, key_width), F32),
            pltpu.VMEM((unroll,chunk, key_width), F32),
            pltpu.VMEM((unroll,chunk, key_width), F32),
        ],
        compiler_params=_cparams(("arbitrary", "arbitrary")),
        name=f"{mode}_{'bwd' if reverse else 'fwd'}",
    )(*operands)


GLA_CHUNK, GLA_BLOCK = 128, 128
HGRN_CHUNK, HGRN_BLOCK = 64, 16


def _gla_branch(p3, lr, up_pads, gbias_pads, norm_w, bsz, t):
    fwd = _recurrent_pass("gla", False, False, GLA_CHUNK, GLA_BLOCK, p3, bsz, t,
                          (lr, up_pads[0][0], up_pads[0][1], gbias_pads[0]))
    return _recurrent_pass("gla", True, True, GLA_CHUNK, GLA_BLOCK, p3, bsz, t,
                           (lr, up_pads[1][0], up_pads[1][1], gbias_pads[1]), prev=fwd, norm_w=norm_w)


def _hgrn_branch(p3, lbs, norm_w, bsz, t):
    fwd = _recurrent_pass("hgrn", False, False, HGRN_CHUNK, HGRN_BLOCK, p3, bsz, t, (lbs[0],))
    return _recurrent_pass("hgrn", True, True, HGRN_CHUNK, HGRN_BLOCK, p3, bsz, t, (lbs[1],),
                           prev=fwd, norm_w=norm_w)


MLP_ROWS = 512
MLP_GROUP_ROWS = 256


def _mlp_kernel(x_ref, ona_ref, ogla_ref, ohg_ref, gates_ref, mod_ref, wna_ref, wgla_ref, whg_ref, wout_ref,
                n2_ref, wg_ref, wu_ref, wd_ref, fn_ref, o_ref, *, last):
    d = D_MODEL
    gate1 = mod_ref[0, :, 2 * d:3 * d]
    shift2 = mod_ref[0, :, 3 * d:4 * d]
    scale2 = mod_ref[0, :, 4 * d:5 * d]
    gate2 = mod_ref[0, :, 5 * d:6 * d]
    groups = [slice(i * MLP_GROUP_ROWS, (i + 1) * MLP_GROUP_ROWS) for i in range(MLP_ROWS // MLP_GROUP_ROWS)]

    def gate(rs, i):
        g = jnp.concatenate([gates_ref[2 * i, rs, :].astype(F32), gates_ref[2 * i + 1, rs, :].astype(F32)], axis=1)
        return _sigmoid(g)

    merged = [(gate(rs, 0) * _dot(ona_ref[rs, :], wna_ref[0])
               + gate(rs, 1) * _dot(ogla_ref[rs, :], wgla_ref[0])
               + gate(rs, 2) * _dot(ohg_ref[rs, :], whg_ref[0])).astype(BF16) for rs in groups]
    x1 = [x_ref[rs, :] + gate1 * _dot(m, wout_ref[0]) for rs, m in zip(groups, merged)]
    h = [_rmsnorm_mod(x, n2_ref[0], scale2, shift2).astype(BF16) for x in x1]
    act = [(_silu(_dot(hh, wg_ref[0])) * _dot(hh, wu_ref[0])).astype(BF16) for hh in h]
    for rs, x, a in zip(groups, x1, act):
        x2 = x + gate2 * _dot(a, wd_ref[0])
        if last:
            ms = jnp.mean(x2 * x2, axis=-1, keepdims=True)
            x2 = x2 * lax.rsqrt(ms + RMS_EPS) * fn_ref[...]
        o_ref[rs, :] = x2


def _merge_mlp(x2, o_na, o_gla, o_hg, p3, mod_l, layer_wts, final_norm_w, layer, tokens_per_batch, last):
    n, d = x2.shape
    tm = MLP_ROWS
    per_batch = tokens_per_batch // tm
    row = lambda w: pl.BlockSpec((tm, w), lambda i: (i, 0))
    return pl.pallas_call(
        functools.partial(_mlp_kernel, last=last),
        grid=(n // tm,),
        in_specs=[
            row(d), row(NA_WIDTH), row(NA_WIDTH), row(NA_WIDTH),
            pl.BlockSpec((N_GATE_CHUNKS, tm, PCHUNK), lambda i: (CH_GATES // N_GATE_CHUNKS, i, 0)),
            pl.BlockSpec((1, 1, N_MOD * d), lambda i: (i // per_batch, 0, 0)),
        ] + [_layer_resident(w.shape, layer) for w in layer_wts] + [_resident(final_norm_w.shape)],
        out_specs=row(d),
        out_shape=jax.ShapeDtypeStruct((n, d), F32),
        compiler_params=_cparams(("arbitrary",)),
        name="merge_mlp",
    )(x2, o_na, o_gla, o_hg, p3, mod_l, *layer_wts, final_norm_w)


def _transposed_w_in(w):
    assert w.shape[2] == N_PCHUNKS * PCHUNK + 2 * GLA_RANK
    return jnp.swapaxes(w, 1, 2).astype(BF16)


def _query_scales():
    qs = np.ones((N_HEAD_CHUNKS, 1, PCHUNK), np.float32)
    qs[CH_NA_Q] = NA_HEAD_DIM ** -0.5
    qs[CH_GLA_QK, :, :GLA_HEADS * GLA_DK] = GLA_DK ** -0.5
    return jnp.asarray(qs)


def kernel(x, c, w_ada, b_ada, norm1_w, w_in, na_rpb, gla_lr_up, gla_lr_bias, gla_norm_w, hgrn_lb_logits, hgrn_norm_w, w_proj_na, w_proj_gla, w_proj_hgrn, w_out, norm2_w, w_ffn_gate, w_ffn_up, w_ffn_down, final_norm_w):
    bsz, t, d = x.shape
    depth = w_in.shape[0]
    assert d == D_MODEL and t % REC_TOKENS_PER_STEP == 0 and t % INPROJ_ROWS == 0
    n = bsz * t

    mods = _modulation(c, w_ada, b_ada).reshape(depth, bsz, 1, N_MOD * d)
    lb_p = jax.nn.softmax(hgrn_lb_logits.astype(F32), axis=0)
    lb_all = jnp.clip(jnp.cumsum(lb_p, axis=0) - lb_p[0], 0.0, 1.0)

    w_t = _transposed_w_in(w_in)
    q_scale = _query_scales()
    bias_tabs = _na_bias_tables(na_rpb, t // GRID_W)
    norm1 = norm1_w.reshape(depth, 1, d)
    mlp_wts = [w_proj_na.astype(BF16), w_proj_gla.astype(BF16), w_proj_hgrn.astype(BF16), w_out.astype(BF16),
               norm2_w.reshape(depth, 1, d), w_ffn_gate.astype(BF16), w_ffn_up.astype(BF16),
               w_ffn_down.astype(BF16)]
    up = jnp.zeros((depth, 2, LR_PAD, GLA_HEADS * GLA_DK), F32)
    for s in range(2):
        up = up.at[:, s, s * GLA_RANK:(s + 1) * GLA_RANK].set(gla_lr_up[:, s])
    up_hi = up.astype(BF16)
    up_lo = (up - up_hi.astype(F32)).astype(BF16)

    xf = x.reshape(n, d)
    for l in range(depth):
        p3, lr = _in_projection(xf, mods[l], norm1, q_scale, w_t, l, t)

        o_na = _neighborhood_attention(p3, bias_tabs, l, bsz, t)

        up_pads = [(up_hi[l, s], up_lo[l, s]) for s in range(2)]
        gbias = [gla_lr_bias[l, s].reshape(1, -1) for s in range(2)]
        o_gla = _gla_branch(p3, lr, up_pads, gbias, gla_norm_w[l].reshape(1, -1), bsz, t)

        lbs = [lb_all[l, s * HGRN_KEY_WIDTH:(s + 1) * HGRN_KEY_WIDTH].reshape(1, -1) for s in range(2)]
        o_hg = _hgrn_branch(p3, lbs, hgrn_norm_w[l].reshape(1, -1), bsz, t)

        xf = _merge_mlp(xf, o_na, o_gla, o_hg, p3, mods[l], mlp_wts, final_norm_w.reshape(1, d), l, t,
                        last=(l == depth - 1))
    return xf.reshape(bsz, t, d)
```

```python
import functools

import jax
import jax.numpy as jnp
import numpy as np
from jax import lax
from jax.experimental import pallas as pl
from jax.experimental.pallas import tpu as pltpu

F32 = jnp.float32
BF16 = jnp.bfloat16

D_MODEL = 1024
GRID_W = 64
WIN_R = 8
WIN_C = 16
NA_HEADS = 8
NA_HEAD_DIM = 64
NA_WIDTH = 512
GLA_HEADS = 4
GLA_DK = 64
GLA_RANK = 16
GLA_TAU = 16.0
HGRN_KEY_WIDTH = 512
N_MOD = 6
FFN_HIDDEN = 2816
RMS_EPS = 1e-6
NEG_INF = -1e30
F_FLOOR = 1e-30

PCHUNK = 512
N_GATE_CHUNKS = 6
N_HEAD_CHUNKS = 6
N_TAIL_CHUNKS = 11
CH_NA_Q, CH_NA_K, CH_NA_V, CH_GLA_QK, CH_GLA_V, CH_GLA_G = range(N_HEAD_CHUNKS)
CH_GATES = N_HEAD_CHUNKS
CH_HG_Q, CH_HG_FF, CH_HG_FB, CH_HG_I, CH_HG_G = range(CH_GATES + N_GATE_CHUNKS, CH_GATES + N_GATE_CHUNKS + 5)
N_PCHUNKS = N_HEAD_CHUNKS + N_TAIL_CHUNKS
LR_PAD = 128
HEAD_LANES = 128
SUBLANES = 8
N_REC_HEADS = 4

VMEM_LIMIT_BYTES = 56 * 1024 * 1024

REC_TOKENS_PER_STEP = 1024
REC_MAX_UNROLL = 8
DECAY_SPAN_LIMIT = 60.0


def _cparams(sem):
    return pltpu.CompilerParams(dimension_semantics=sem, vmem_limit_bytes=VMEM_LIMIT_BYTES)


def _resident(shape):
    nd = len(shape)
    return pl.BlockSpec(shape, lambda *_: (0,) * nd, pipeline_mode=pl.Buffered(1))


def _layer_resident(shape, layer):
    nd = len(shape)
    return pl.BlockSpec((1,) + tuple(shape[1:]), lambda *_: (layer,) + (0,) * (nd - 1), pipeline_mode=pl.Buffered(1))


def _dot(a, b):
    return jnp.dot(a, b, preferred_element_type=F32)


def _dot_nt(a, b):
    return lax.dot_general(a, b, (((1,), (1,)), ((), ())), preferred_element_type=F32)


def _dot_tn(a, b):
    return lax.dot_general(a, b, (((0,), (0,)), ((), ())), preferred_element_type=F32)


def _sigmoid_pair(z):
    e = jnp.exp(-jnp.abs(z))
    r = 1.0 / (1.0 + e)
    er = e * r
    pos = z >= 0
    return jnp.where(pos, r, er), jnp.where(pos, er, r)


def _sigmoid(z):
    return 0.5 * jnp.tanh(0.5 * z) + 0.5


def _silu(z):
    return z * _sigmoid(z)


def _split_bf16(a):
    hi = a.astype(BF16)
    lo = (a - hi.astype(F32)).astype(BF16)
    return hi, lo


MOD_COLS = 1536


def _mod_kernel(c_ref, w_ref, b_ref, o_ref):
    c_act = _silu(c_ref[...])
    c_hi, c_lo = _split_bf16(c_act)
    w_hi, w_lo = _split_bf16(w_ref[0])
    acc = _dot(c_hi, w_hi) + _dot(c_lo, w_hi) + _dot(c_hi, w_lo)
    o_ref[0] = acc + b_ref[0]


def _modulation(c, w_ada, b_ada):
    depth, d, width = w_ada.shape
    bsz = c.shape[0]
    rows = 8
    c_pad = jnp.zeros((rows, d), F32).at[:bsz].set(c)
    out = pl.pallas_call(
        _mod_kernel,
        grid=(depth, width // MOD_COLS),
        in_specs=[
            pl.BlockSpec((rows, d), lambda l, j: (0, 0)),
            pl.BlockSpec((1, d, MOD_COLS), lambda l, j: (l, 0, j)),
            pl.BlockSpec((1, 1, MOD_COLS), lambda l, j: (l, 0, j)),
        ],
        out_specs=pl.BlockSpec((1, rows, MOD_COLS), lambda l, j: (l, 0, j)),
        out_shape=jax.ShapeDtypeStruct((depth, rows, width), F32),
        compiler_params=_cparams(("arbitrary", "arbitrary")),
        name="adaln_mod",
    )(c_pad, w_ada, b_ada.reshape(depth, 1, width))
    return out[:, :bsz]


INPROJ_ROWS = 512
INPROJ_NORM_AFTER = (1, 3)


def _rmsnorm_mod(x, w, scale, shift):
    ms = jnp.mean(x * x, axis=-1, keepdims=True)
    y = x * lax.rsqrt(ms + RMS_EPS) * w
    return y * (1.0 + scale) + shift


def _inproj_kernel(x_ref, xn_ref, mod_ref, modn_ref, nw_ref, qs_ref, w_ref, p_ref, lr_ref, h_ref):
    d = D_MODEL
    step = pl.program_id(0)
    cur = step % 2
    lr0 = N_HEAD_CHUNKS * PCHUNK
    tail0 = lr0 + 2 * GLA_RANK

    def normed(xr, mr, rows):
        return _rmsnorm_mod(xr[rows, :], nw_ref[0], mr[0, :, d:2 * d], mr[0, :, 0:d]).astype(BF16)

    @pl.when(step == 0)
    def _():
        h_ref[0] = normed(x_ref, mod_ref, slice(None))

    half = INPROJ_ROWS // 2
    for j in range(N_PCHUNKS):
        if j < N_HEAD_CHUNKS:
            p_ref[j] = (_dot_nt(h_ref[cur], w_ref[0, j * PCHUNK:(j + 1) * PCHUNK, :]) * qs_ref[j]).astype(BF16)
        else:
            t = j - N_HEAD_CHUNKS
            dst = CH_HG_Q + t if t < N_TAIL_CHUNKS - N_GATE_CHUNKS else CH_GATES + t - (N_TAIL_CHUNKS - N_GATE_CHUNKS)
            p_ref[dst] = _dot_nt(h_ref[cur], w_ref[0, tail0 + t * PCHUNK:tail0 + (t + 1) * PCHUNK, :]).astype(BF16)
        if j in INPROJ_NORM_AFTER:
            rows = slice(0, half) if j == INPROJ_NORM_AFTER[0] else slice(half, INPROJ_ROWS)
            h_ref[1 - cur, rows, :] = normed(xn_ref, modn_ref, rows)
    lr = _dot_nt(h_ref[cur], w_ref[0, lr0:tail0, :])
    lr_ref[...] = jnp.concatenate([lr, jnp.zeros((INPROJ_ROWS, LR_PAD - 2 * GLA_RANK), F32)], axis=1).astype(BF16)


def _in_projection(x2, mod_l, norm_w, q_scale, w_t, layer, tokens_per_batch):
    n, d = x2.shape
    tm = INPROJ_ROWS
    per_batch = tokens_per_batch // tm
    last = n // tm - 1
    nxt = lambda i: jnp.minimum(i + 1, last)
    return pl.pallas_call(
        _inproj_kernel,
        grid=(n // tm,),
        in_specs=[
            pl.BlockSpec((tm, d), lambda i: (i, 0)),
            pl.BlockSpec((tm, d), lambda i: (nxt(i), 0)),
            pl.BlockSpec((1, 1, N_MOD * d), lambda i: (i // per_batch, 0, 0)),
            pl.BlockSpec((1, 1, N_MOD * d), lambda i: (nxt(i) // per_batch, 0, 0)),
            _layer_resident(norm_w.shape, layer),
            _resident(q_scale.shape),
            _layer_resident(w_t.shape, layer),
        ],
        out_specs=[
            pl.BlockSpec((N_PCHUNKS, tm, PCHUNK), lambda i: (0, i, 0)),
            pl.BlockSpec((tm, LR_PAD), lambda i: (i, 0)),
        ],
        out_shape=[
            jax.ShapeDtypeStruct((N_PCHUNKS, n, PCHUNK), BF16),
            jax.ShapeDtypeStruct((n, LR_PAD), BF16),
        ],
        scratch_shapes=[pltpu.VMEM((2, tm, d), BF16)],
        compiler_params=_cparams(("arbitrary",)),
        name="in_proj",
    )(x2, x2, mod_l, mod_l, norm_w, q_scale, w_t)


NA_ROWS_PER_STEP = 8
NA_ROW_UNROLL = 4
NA_BAND = WIN_R * GRID_W
NA_VARIANTS = 8


def _na_kernel(q_ref, k_ref, v_ref, bias_ref, o_ref, *, rows_total):
    step = pl.program_id(1)
    lane = lax.broadcasted_iota(jnp.int32, (GRID_W, HEAD_LANES), 1)
    low = lane < NA_HEAD_DIM

    def row_group(gi, carry):
        items = []
        for u in range(NA_ROW_UNROLL):
            rr = gi * NA_ROW_UNROLL + u
            r = step * NA_ROWS_PER_STEP + rr
            row_start = jnp.clip(r - WIN_R // 2, 0, rows_total - WIN_R)
            variant = jnp.where(r < WIN_R // 2, r,
                                jnp.where(r > rows_total - WIN_R // 2, r - (rows_total - WIN_R), WIN_R // 2))
            k0 = pl.multiple_of(row_start * GRID_W, GRID_W)
            q0 = pl.multiple_of(rr * GRID_W, GRID_W)
            for hp in range(NA_HEADS // 2):
                items.append((q0, k0, variant, hp, slice(hp * HEAD_LANES, (hp + 1) * HEAD_LANES)))
        scores = []
        for q0, k0, variant, hp, hs in items:
            qp = q_ref[0, pl.ds(q0, GRID_W), hs]
            zero = jnp.zeros_like(qp)
            q2 = jnp.concatenate([jnp.where(low, qp, zero), jnp.where(low, zero, qp)], axis=0)
            scores.append(_dot_nt(q2, k_ref[0, pl.ds(k0, NA_BAND), hs]) + bias_ref[0, variant, hp])
        probs = []
        for s in scores:
            e = jnp.exp(s - jnp.max(s, axis=-1, keepdims=True))
            probs.append((e.astype(BF16), jnp.sum(e, axis=-1, keepdims=True)))
        for (q0, k0, variant, hp, hs), (e16, denom) in zip(items, probs):
            o2 = _dot(e16, v_ref[0, pl.ds(k0, NA_BAND), hs]) / denom
            o_ref[pl.ds(q0, GRID_W), hs] = jnp.where(low, o2[:GRID_W], o2[GRID_W:]).astype(BF16)
        return carry

    lax.fori_loop(0, NA_ROWS_PER_STEP // NA_ROW_UNROLL, row_group, 0)


def _na_bias_tables(rpb, rows):
    depth = rpb.shape[0]
    wr = min(WIN_R, rows)
    col = np.arange(GRID_W)
    col_start = np.clip(col - WIN_C // 2, 0, GRID_W - WIN_C)
    col_mask = (col[None, :] >= col_start[:, None]) & (col[None, :] < col_start[:, None] + WIN_C)
    col_off = np.clip(col[None, :] - col[:, None], -(WIN_C - 1), WIN_C - 1) + (WIN_C - 1)
    n_ro, n_co = 2 * WIN_R - 1, 2 * WIN_C - 1
    onehot = (col_off.reshape(-1)[None, :] == np.arange(n_co)[:, None]).astype(np.float32)
    by_col = jnp.dot(rpb.astype(F32).reshape(depth * NA_HEADS * n_ro, n_co), onehot, precision=lax.Precision.HIGHEST)
    by_col = by_col.reshape(depth, NA_HEADS, n_ro, GRID_W, GRID_W)
    by_col = jnp.where(col_mask[None, None, None], by_col, NEG_INF)
    rep_rows = list(range(WIN_R // 2)) + [WIN_R // 2] + [rows - WIN_R + v for v in range(WIN_R // 2 + 1, WIN_R)]
    tabs = []
    for r in rep_rows:
        row_start = int(np.clip(r - WIN_R // 2, 0, rows - wr))
        ro0 = row_start - r + (WIN_R - 1)
        tabs.append(by_col[:, :, ro0:ro0 + wr])
    tabs = jnp.stack(tabs, axis=1).transpose(0, 1, 2, 4, 3, 5)
    return tabs.reshape(depth, NA_VARIANTS, NA_HEADS // 2, 2 * GRID_W, wr * GRID_W)


def _neighborhood_attention(p3, bias_tabs, layer, bsz, t):
    rows = t // GRID_W
    assert rows >= 2 * WIN_R and rows % NA_ROWS_PER_STEP == 0
    n = bsz * t
    tq = NA_ROWS_PER_STEP * GRID_W
    steps = t // tq
    return pl.pallas_call(
        functools.partial(_na_kernel, rows_total=rows),
        grid=(bsz, steps),
        in_specs=[
            pl.BlockSpec((1, tq, PCHUNK), lambda b, i: (CH_NA_Q, b * steps + i, 0)),
            pl.BlockSpec((1, t, PCHUNK), lambda b, i: (CH_NA_K, b, 0)),
            pl.BlockSpec((1, t, PCHUNK), lambda b, i: (CH_NA_V, b, 0)),
            _layer_resident(bias_tabs.shape, layer),
        ],
        out_specs=pl.BlockSpec((tq, NA_WIDTH), lambda b, i: (b * steps + i, 0)),
        out_shape=jax.ShapeDtypeStruct((n, NA_WIDTH), BF16),
        compiler_params=_cparams(("arbitrary", "arbitrary")),
        name="neighborhood_attn",
    )(p3, p3, p3, bias_tabs)


def _rec_kernel(*refs, mode, reverse, final, chunk, block, unroll):
    it = iter(refs)
    q_ref = next(it)
    if mode == "gla":
        v_ref, lr_ref, uph_ref, upl_ref, gbias_ref = (next(it) for _ in range(5))
    else:
        f_ref, v_ref, lb_ref = (next(it) for _ in range(3))
    tri_ref = next(it)
    if final:
        prev_ref, gate_ref, nw_ref = (next(it) for _ in range(3))
    o_ref = next(it)
    st_ref, st_old, kbuf, wbuf, qbuf = (next(it) for _ in range(5))

    c, s = chunk, block
    nb = c // s
    dk = GLA_DK if mode == "gla" else HEAD_LANES
    width = N_REC_HEADS * dk
    n_chunks = REC_TOKENS_PER_STEP // c

    def key_lanes(h):
        if dk == HEAD_LANES:
            return slice(h * HEAD_LANES, (h + 1) * HEAD_LANES), None
        t = (h * dk) // HEAD_LANES
        lane = lax.broadcasted_iota(jnp.int32, (1, HEAD_LANES), 1)
        mine = (lane // dk) == (h - t * (HEAD_LANES // dk))
        return slice(t * HEAD_LANES, (t + 1) * HEAD_LANES), mine

    def own(x, mine):
        return x if mine is None else jnp.where(mine, x, jnp.zeros_like(x))

    @pl.when(pl.program_id(1) == 0)
    def _():
        st_ref[...] = jnp.zeros_like(st_ref)

    ii = lax.broadcasted_iota(jnp.int32, (c, c), 0)
    jj = lax.broadcasted_iota(jnp.int32, (c, c), 1)
    bi, bj = ii // s, jj // s
    gap = (bj - bi) if reverse else (bi - bj)
    seen = (jj >= ii) if reverse else (jj <= ii)
    cls = jnp.where(gap == 0, jnp.where(seen, 0, -1), jnp.where(gap > 0, gap, -1))
    row_id = lax.broadcasted_iota(jnp.int32, (c, 1), 0)
    col_id = lax.broadcasted_iota(jnp.int32, (1, c), 1)

    def prev_block(i, steps=1):
        return i + steps if reverse else i - steps

    def per_block(vals):
        return jnp.concatenate([jnp.broadcast_to(x, (s, width)) for x in vals], axis=0)

    def chunk_inputs(rows):
        if mode == "gla":
            q = q_ref[0, rows, :width].astype(F32)
            k = q_ref[0, rows, width:].astype(F32)
            lr = lr_ref[rows, :]
            z = _dot(lr, uph_ref[...]) + _dot(lr, upl_ref[...]) + gbias_ref[...]
            g = (jnp.minimum(z, 0.0) - jnp.log(1.0 + jnp.exp(-jnp.abs(z)))) * (1.0 / GLA_TAU)
        else:
            z = f_ref[0, rows, :].astype(F32)
            sig, sig_neg = _sigmoid_pair(z)
            lb = lb_ref[...]
            g = jnp.log(jnp.maximum(lb + (1.0 - lb) * sig, F_FLOOR))
            k = (1.0 - lb) * sig_neg
            q = _silu(q_ref[0, rows, :].astype(F32))
        return q, k, g

    def chunk_factors(q, k, w, exact):
        w3 = w.reshape(nb, s, width)
        last = 0 if reverse else s - 1
        wb = [w3[i, last:last + 1, :] for i in range(nb)]
        order = list(range(nb - 1, -1, -1)) if reverse else list(range(nb))
        beta = [None] * nb
        run = jnp.zeros((1, width), F32)
        for i in order:
            beta[i] = run
            run = run + wb[i]
        total = run
        ewb = [jnp.exp(x) for x in wb]
        qd = q * jnp.exp(w)
        ks = k * jnp.exp(per_block(wb) - w)
        kd16 = None if exact else (ks * per_block([jnp.exp(-x) for x in wb])).astype(BF16)
        one = jnp.ones((1, width), F32)
        lhs, fac = [qd.astype(BF16)], [one] * nb
        for dgap in range(2, nb):
            fac = [fac[i] * (ewb[prev_block(i, dgap - 1)] if 0 <= prev_block(i, dgap - 1) < nb else one)
                   for i in range(nb)]
            lhs.append((qd * per_block(fac)).astype(BF16))
        ks16 = ks.astype(BF16)
        if nb == 1:
            return lhs, kd16, ks16, lhs[0], ks16, jnp.exp(total)
        qe16 = (qd * jnp.exp(per_block(beta))).astype(BF16)
        ke16 = (ks * jnp.exp(per_block([total - beta[i] - wb[i] for i in range(nb)]))).astype(BF16)
        return lhs, kd16, ks16, qe16, ke16, jnp.exp(total)

    def head_scores(h, lhs, kd16, ks16, sd):
        ks_, mine = key_lanes(h)
        if sd is None:
            sd = _dot_nt(own(lhs[0][:, ks_], mine), kd16[:, ks_])
        a = jnp.where(cls == 0, sd, 0.0)
        if nb > 1:
            def far_rows(dgap):
                return slice(0, c - dgap * s) if reverse else slice(dgap * s, c)

            parts = [own(lhs[max(dgap - 1, 0)][:, ks_], mine)[far_rows(dgap)] for dgap in range(1, nb)]
            so = _dot_nt(jnp.concatenate(parts, axis=0), ks16[:, ks_])
            start = 0
            for dgap in range(1, nb):
                n_far = c - dgap * s
                blank = jnp.zeros((dgap * s, c), F32)
                part = so[start:start + n_far]
                start += n_far
                full = jnp.concatenate([part, blank] if reverse else [blank, part], axis=0)
                a = jnp.where(cls == dgap, full, a)
        return a.astype(BF16)

    def chunk_output(h, a16, qe16, v16, st):
        ks_, mine = key_lanes(h)
        lhs = jnp.concatenate([own(qe16[:, ks_], mine), a16], axis=1)
        return _dot(lhs, jnp.concatenate([st.astype(BF16), v16], axis=0))

    def emit(h, rows, o):
        hs = slice(h * HEAD_LANES, (h + 1) * HEAD_LANES)
        if final:
            tot = prev_ref[rows, hs] + o
            ms = jnp.mean(tot * tot, axis=-1, keepdims=True)
            y = tot * lax.rsqrt(ms + RMS_EPS) * nw_ref[...]
            o_ref[rows, hs] = (y * _silu(gate_ref[0, rows, hs].astype(F32))).astype(o_ref.dtype)
        else:
            o_ref[rows, hs] = o

    def exact_same_block_scores(u, h):
        hs, mine = key_lanes(h)

        def col_body(jg, acc):
            j0 = pl.multiple_of(jg * SUBLANES, SUBLANES)
            k8 = kbuf[u, pl.ds(j0, SUBLANES), hs]
            w8 = wbuf[u, pl.ds(j0, SUBLANES), hs]
            for jr in range(SUBLANES):
                j = j0 + jr
                same = (row_id // s) == (j // s)
                valid = same & ((row_id <= j) if reverse else (row_id >= j))
                e = jnp.exp(jnp.where(valid, wbuf[u, :, hs] - w8[jr:jr + 1], 0.0))
                t = jnp.sum(own(qbuf[u, :, hs], mine) * k8[jr:jr + 1] * e, axis=-1, keepdims=True)
                acc = jnp.where(valid & (col_id == j), t, acc)
            return acc

        return lax.fori_loop(0, c // SUBLANES, col_body, jnp.zeros((c, c), F32))

    def group_body(gi, carry):
        all_rows = []
        for u in range(unroll):
            ci = gi * unroll + u
            cc = (n_chunks - 1 - ci) if reverse else ci
            all_rows.append(pl.ds(pl.multiple_of(cc * c, c), c))
        ws, facs = [], []
        for u, rows in enumerate(all_rows):
            q, k, g = chunk_inputs(rows)
            g_hi, g_lo = _split_bf16(g)
            w2 = _dot(tri_ref[...], jnp.concatenate([g_hi, g_lo], axis=1))
            w = w2[:, :width] + w2[:, width:]
            ws.append(w)
            facs.append(chunk_factors(q, k, w, exact=False))
            qbuf[u] = q
            kbuf[u] = k
            wbuf[u] = w
        a16 = [[head_scores(h, f[0], f[1], f[2], None) for h in range(N_REC_HEADS)] for f in facs]
        vs, incs = [], []
        for u, rows in enumerate(all_rows):
            ke16 = facs[u][4]
            vs.append([v_ref[0, rows, h * HEAD_LANES:(h + 1) * HEAD_LANES] for h in range(N_REC_HEADS)])
            incs.append([_dot_tn(own(ke16[:, key_lanes(h)[0]], key_lanes(h)[1]), vs[u][h])
                         for h in range(N_REC_HEADS)])
        states = []
        for h in range(N_REC_HEADS):
            ks_, _ = key_lanes(h)
            st = st_ref[h]
            states.append([])
            for u in range(unroll):
                states[h].append(st)
                st_old[u, h] = st
                decay_col = jnp.broadcast_to(facs[u][5][:, ks_], (HEAD_LANES, HEAD_LANES)).T
                st = st * decay_col + incs[u][h]
            st_ref[h] = st
        for u, rows in enumerate(all_rows):
            for h in range(N_REC_HEADS):
                emit(h, rows, chunk_output(h, a16[u][h], facs[u][3], vs[u][h], states[h][u]))

        w_min = functools.reduce(jnp.minimum, ws)

        @pl.when(jnp.min(w_min) < -DECAY_SPAN_LIMIT)
        def _():
            for u, rows in enumerate(all_rows):
                lhs, _, ks16, qe16, _, _ = chunk_factors(qbuf[u], kbuf[u], wbuf[u], exact=True)
                for h in range(N_REC_HEADS):
                    hs = slice(h * HEAD_LANES, (h + 1) * HEAD_LANES)
                    a = head_scores(h, lhs, None, ks16, exact_same_block_scores(u, h))
                    emit(h, rows, chunk_output(h, a, qe16, v_ref[0, rows, hs], st_old[u, h]))

        return carry

    lax.fori_loop(0, n_chunks // unroll, group_body, 0)


def _scan_mask(chunk, block, reverse):
    i = np.arange(chunk)
    same = (i[:, None] // block) == (i[None, :] // block)
    seen = (i[None, :] >= i[:, None]) if reverse else (i[None, :] <= i[:, None])
    return jnp.asarray((same & seen).astype(np.float32), BF16)


def _recurrent_pass(mode, reverse, final, chunk, block, p3, bsz, t, extra, prev=None, norm_w=None):
    n = bsz * t
    tb = REC_TOKENS_PER_STEP
    nblk = t // tb
    width = N_REC_HEADS * HEAD_LANES
    key_width = N_REC_HEADS * (GLA_DK if mode == "gla" else HEAD_LANES)
    unroll = min(tb // chunk, REC_MAX_UNROLL)

    def tok(b, i):
        return b * nblk + ((nblk - 1 - i) if reverse else i)

    def chunk_spec(ch):
        return pl.BlockSpec((1, tb, PCHUNK), lambda b, i: (ch, tok(b, i), 0))

    tok_spec = lambda w: pl.BlockSpec((tb, w), lambda b, i: (tok(b, i), 0))
    operands, specs = [], []
    if mode == "gla":
        lr, up_hi, up_lo, gbias = extra
        operands += [p3, p3, lr, up_hi, up_lo, gbias]
        specs += [chunk_spec(CH_GLA_QK), chunk_spec(CH_GLA_V), tok_spec(LR_PAD),
                  _resident(up_hi.shape), _resident(up_lo.shape), _resident(gbias.shape)]
        gate_ch = CH_GLA_G
    else:
        (lb,) = extra
        operands += [p3, p3, p3, lb]
        specs += [chunk_spec(CH_HG_Q), chunk_spec(CH_HG_FB if reverse else CH_HG_FF), chunk_spec(CH_HG_I),
                  _resident(lb.shape)]
        gate_ch = CH_HG_G
    tri = _scan_mask(chunk, block, reverse)
    operands.append(tri)
    specs.append(_resident(tri.shape))
    if final:
        operands += [prev, p3, norm_w]
        specs += [tok_spec(width), chunk_spec(gate_ch), _resident(norm_w.shape)]
    out_dtype = BF16 if final else F32
    return pl.pallas_call(
        functools.partial(_rec_kernel, mode=mode, reverse=reverse, final=final, chunk=chunk, block=block,
                          unroll=unroll),
        grid=(bsz, nblk),
        in_specs=specs,
        out_specs=tok_spec(width),
        out_shape=jax.ShapeDtypeStruct((n, width), out_dtype),
        scratch_shapes=[
            pltpu.VMEM((N_REC_HEADS, HEAD_LANES, HEAD_LANES), F32),
            pltpu.VMEM((unroll,N_REC_HEADS, HEAD_LANES, HEAD_LANES), F32),
            pltpu.VMEM((unroll,chunk, key_width), F32),
            pltpu.VMEM((unroll,chunk, key_width), F32),
            pltpu.VMEM((unroll,chunk, key_width), F32),
        ],
        compiler_params=_cparams(("arbitrary", "arbitrary")),
        name=f"{mode}_{'bwd' if reverse else 'fwd'}",
    )(*operands)


GLA_CHUNK, GLA_BLOCK = 128, 128
HGRN_CHUNK, HGRN_BLOCK = 64, 16


def _gla_branch(p3, lr, up_pads, gbias_pads, norm_w, bsz, t):
    fwd = _recurrent_pass("gla", False, False, GLA_CHUNK, GLA_BLOCK, p3, bsz, t,
                          (lr, up_pads[0][0], up_pads[0][1], gbias_pads[0]))
    return _recurrent_pass("gla", True, True, GLA_CHUNK, GLA_BLOCK, p3, bsz, t,
                           (lr, up_pads[1][0], up_pads[1][1], gbias_pads[1]), prev=fwd, norm_w=norm_w)


def _hgrn_branch(p3, lbs, norm_w, bsz, t):
    fwd = _recurrent_pass("hgrn", False, False, HGRN_CHUNK, HGRN_BLOCK, p3, bsz, t, (lbs[0],))
    return _recurrent_pass("hgrn", True, True, HGRN_CHUNK, HGRN_BLOCK, p3, bsz, t, (lbs[1],),
                           prev=fwd, norm_w=norm_w)


MLP_ROWS = 512
MLP_GROUP_ROWS = 256


def _mlp_kernel(x_ref, ona_ref, ogla_ref, ohg_ref, gates_ref, mod_ref, wna_ref, wgla_ref, whg_ref, wout_ref,
                n2_ref, wg_ref, wu_ref, wd_ref, fn_ref, o_ref, *, last):
    d = D_MODEL
    gate1 = mod_ref[0, :, 2 * d:3 * d]
    shift2 = mod_ref[0, :, 3 * d:4 * d]
    scale2 = mod_ref[0, :, 4 * d:5 * d]
    gate2 = mod_ref[0, :, 5 * d:6 * d]
    groups = [slice(i * MLP_GROUP_ROWS, (i + 1) * MLP_GROUP_ROWS) for i in range(MLP_ROWS // MLP_GROUP_ROWS)]

    def gate(rs, i):
        g = jnp.concatenate([gates_ref[2 * i, rs, :].astype(F32), gates_ref[2 * i + 1, rs, :].astype(F32)], axis=1)
        return _sigmoid(g)

    merged = [(gate(rs, 0) * _dot(ona_ref[rs, :], wna_ref[0])
               + gate(rs, 1) * _dot(ogla_ref[rs, :], wgla_ref[0])
               + gate(rs, 2) * _dot(ohg_ref[rs, :], whg_ref[0])).astype(BF16) for rs in groups]
    x1 = [x_ref[rs, :] + gate1 * _dot(m, wout_ref[0]) for rs, m in zip(groups, merged)]
    h = [_rmsnorm_mod(x, n2_ref[0], scale2, shift2).astype(BF16) for x in x1]
    act = [(_silu(_dot(hh, wg_ref[0])) * _dot(hh, wu_ref[0])).astype(BF16) for hh in h]
    for rs, x, a in zip(groups, x1, act):
        x2 = x + gate2 * _dot(a, wd_ref[0])
        if last:
            ms = jnp.mean(x2 * x2, axis=-1, keepdims=True)
            x2 = x2 * lax.rsqrt(ms + RMS_EPS) * fn_ref[...]
        o_ref[rs, :] = x2


def _merge_mlp(x2, o_na, o_gla, o_hg, p3, mod_l, layer_wts, final_norm_w, layer, tokens_per_batch, last):
    n, d = x2.shape
    tm = MLP_ROWS
    per_batch = tokens_per_batch // tm
    row = lambda w: pl.BlockSpec((tm, w), lambda i: (i, 0))
    return pl.pallas_call(
        functools.partial(_mlp_kernel, last=last),
        grid=(n // tm,),
        in_specs=[
            row(d), row(NA_WIDTH), row(NA_WIDTH), row(NA_WIDTH),
            pl.BlockSpec((N_GATE_CHUNKS, tm, PCHUNK), lambda i: (CH_GATES // N_GATE_CHUNKS, i, 0)),
            pl.BlockSpec((1, 1, N_MOD * d), lambda i: (i // per_batch, 0, 0)),
        ] + [_layer_resident(w.shape, layer) for w in layer_wts] + [_resident(final_norm_w.shape)],
        out_specs=row(d),
        out_shape=jax.ShapeDtypeStruct((n, d), F32),
        compiler_params=_cparams(("arbitrary",)),
        name="merge_mlp",
    )(x2, o_na, o_gla, o_hg, p3, mod_l, *layer_wts, final_norm_w)


def _transposed_w_in(w):
    assert w.shape[2] == N_PCHUNKS * PCHUNK + 2 * GLA_RANK
    return jnp.swapaxes(w, 1, 2).astype(BF16)


def _query_scales():
    qs = np.ones((N_HEAD_CHUNKS, 1, PCHUNK), np.float32)
    qs[CH_NA_Q] = NA_HEAD_DIM ** -0.5
    qs[CH_GLA_QK, :, :GLA_HEADS * GLA_DK] = GLA_DK ** -0.5
    return jnp.asarray(qs)


def kernel(x, c, w_ada, b_ada, norm1_w, w_in, na_rpb, gla_lr_up, gla_lr_bias, gla_norm_w, hgrn_lb_logits, hgrn_norm_w, w_proj_na, w_proj_gla, w_proj_hgrn, w_out, norm2_w, w_ffn_gate, w_ffn_up, w_ffn_down, final_norm_w):
    bsz, t, d = x.shape
    depth = w_in.shape[0]
    assert d == D_MODEL and t % REC_TOKENS_PER_STEP == 0 and t % INPROJ_ROWS == 0
    n = bsz * t

    mods = _modulation(c, w_ada, b_ada).reshape(depth, bsz, 1, N_MOD * d)
    lb_p = jax.nn.softmax(hgrn_lb_logits.astype(F32), axis=0)
    lb_all = jnp.clip(jnp.cumsum(lb_p, axis=0) - lb_p[0], 0.0, 1.0)

    w_t = _transposed_w_in(w_in)
    q_scale = _query_scales()
    bias_tabs = _na_bias_tables(na_rpb, t // GRID_W)
    norm1 = norm1_w.reshape(depth, 1, d)
    mlp_wts = [w_proj_na.astype(BF16), w_proj_gla.astype(BF16), w_proj_hgrn.astype(BF16), w_out.astype(BF16),
               norm2_w.reshape(depth, 1, d), w_ffn_gate.astype(BF16), w_ffn_up.astype(BF16),
               w_ffn_down.astype(BF16)]
    up = jnp.zeros((depth, 2, LR_PAD, GLA_HEADS * GLA_DK), F32)
    for s in range(2):
        up = up.at[:, s, s * GLA_RANK:(s + 1) * GLA_RANK].set(gla_lr_up[:, s])
    up_hi = up.astype(BF16)
    up_lo = (up - up_hi.astype(F32)).astype(BF16)

    xf = x.reshape(n, d)
    for l in range(depth):
        p3, lr = _in_projection(xf, mods[l], norm1, q_scale, w_t, l, t)

        o_na = _neighborhood_attention(p3, bias_tabs, l, bsz, t)

        up_pads = [(up_hi[l, s], up_lo[l, s]) for s in range(2)]
        gbias = [gla_lr_bias[l, s].reshape(1, -1) for s in range(2)]
        o_gla = _gla_branch(p3, lr, up_pads, gbias, gla_norm_w[l].reshape(1, -1), bsz, t)

        lbs = [lb_all[l, s * HGRN_KEY_WIDTH:(s + 1) * HGRN_KEY_WIDTH].reshape(1, -1) for s in range(2)]
        o_hg = _hgrn_branch(p3, lbs, hgrn_norm_w[l].reshape(1, -1), bsz, t)

        xf = _merge_mlp(xf, o_na, o_gla, o_hg, p3, mods[l], mlp_wts, final_norm_w.reshape(1, d), l, t,
                        last=(l == depth - 1))
    return xf.reshape(bsz, t, d)
```

```python
import functools

import jax
import jax.numpy as jnp
import numpy as np
from jax import lax
from jax.experimental import pallas as pl
from jax.experimental.pallas import tpu as pltpu

F32 = jnp.float32
BF16 = jnp.bfloat16

D_MODEL = 1024
GRID_W = 64
WIN_R = 8
WIN_C = 16
NA_HEADS = 8
NA_HEAD_DIM = 64
NA_WIDTH = 512
GLA_HEADS = 4
GLA_DK = 64
GLA_RANK = 16
GLA_TAU = 16.0
HGRN_KEY_WIDTH = 512
N_MOD = 6
FFN_HIDDEN = 2816
RMS_EPS = 1e-6
NEG_INF = -1e30
F_FLOOR = 1e-30

PCHUNK = 512
N_GATE_CHUNKS = 6
N_HEAD_CHUNKS = 6
N_TAIL_CHUNKS = 11
CH_NA_Q, CH_NA_K, CH_NA_V, CH_GLA_QK, CH_GLA_V, CH_GLA_G = range(N_HEAD_CHUNKS)
CH_GATES = N_HEAD_CHUNKS
CH_HG_Q, CH_HG_FF, CH_HG_FB, CH_HG_I, CH_HG_G = range(CH_GATES + N_GATE_CHUNKS, CH_GATES + N_GATE_CHUNKS + 5)
N_PCHUNKS = N_HEAD_CHUNKS + N_TAIL_CHUNKS
LR_PAD = 128
HEAD_LANES = 128
SUBLANES = 8
N_REC_HEADS = 4

VMEM_LIMIT_BYTES = 56 * 1024 * 1024

REC_TOKENS_PER_STEP = 1024
REC_MAX_UNROLL = 8
DECAY_SPAN_LIMIT = 60.0


def _cparams(sem):
    return pltpu.CompilerParams(dimension_semantics=sem, vmem_limit_bytes=VMEM_LIMIT_BYTES)


def _resident(shape):
    nd = len(shape)
    return pl.BlockSpec(shape, lambda *_: (0,) * nd, pipeline_mode=pl.Buffered(1))


def _layer_resident(shape, layer):
    nd = len(shape)
    return pl.BlockSpec((1,) + tuple(shape[1:]), lambda *_: (layer,) + (0,) * (nd - 1), pipeline_mode=pl.Buffered(1))


def _dot(a, b):
    return jnp.dot(a, b, preferred_element_type=F32)


def _dot_nt(a, b):
    return lax.dot_general(a, b, (((1,), (1,)), ((), ())), preferred_element_type=F32)


def _dot_tn(a, b):
    return lax.dot_general(a, b, (((0,), (0,)), ((), ())), preferred_element_type=F32)


def _sigmoid_pair(z):
    e = jnp.exp(-jnp.abs(z))
    r = 1.0 / (1.0 + e)
    er = e * r
    pos = z >= 0
    return jnp.where(pos, r, er), jnp.where(pos, er, r)


def _sigmoid(z):
    return 0.5 * jnp.tanh(0.5 * z) + 0.5


def _silu(z):
    return z * _sigmoid(z)


def _split_bf16(a):
    hi = a.astype(BF16)
    lo = (a - hi.astype(F32)).astype(BF16)
    return hi, lo


MOD_COLS = 1536


def _mod_kernel(c_ref, w_ref, b_ref, o_ref):
    c_act = _silu(c_ref[...])
    c_hi, c_lo = _split_bf16(c_act)
    w_hi, w_lo = _split_bf16(w_ref[0])
    acc = _dot(c_hi, w_hi) + _dot(c_lo, w_hi) + _dot(c_hi, w_lo)
    o_ref[0] = acc + b_ref[0]


def _modulation(c, w_ada, b_ada):
    depth, d, width = w_ada.shape
    bsz = c.shape[0]
    rows = 8
    c_pad = jnp.zeros((rows, d), F32).at[:bsz].set(c)
    out = pl.pallas_call(
        _mod_kernel,
        grid=(depth, width // MOD_COLS),
        in_specs=[
            pl.BlockSpec((rows, d), lambda l, j: (0, 0)),
            pl.BlockSpec((1, d, MOD_COLS), lambda l, j: (l, 0, j)),
            pl.BlockSpec((1, 1, MOD_COLS), lambda l, j: (l, 0, j)),
        ],
        out_specs=pl.BlockSpec((1, rows, MOD_COLS), lambda l, j: (l, 0, j)),
        out_shape=jax.ShapeDtypeStruct((depth, rows, width), F32),
        compiler_params=_cparams(("arbitrary", "arbitrary")),
        name="adaln_mod",
    )(c_pad, w_ada, b_ada.reshape(depth, 1, width))
    return out[:, :bsz]


INPROJ_ROWS = 512
INPROJ_NORM_AFTER = (1, 3)


def _rmsnorm_mod(x, w, scale, shift):
    ms = jnp.mean(x * x, axis=-1, keepdims=True)
    y = x * lax.rsqrt(ms + RMS_EPS) * w
    return y * (1.0 + scale) + shift


def _inproj_kernel(x_ref, xn_ref, mod_ref, modn_ref, nw_ref, qs_ref, w_ref, p_ref, lr_ref, h_ref):
    d = D_MODEL
    step = pl.program_id(0)
    cur = step % 2
    lr0 = N_HEAD_CHUNKS * PCHUNK
    tail0 = lr0 + 2 * GLA_RANK

    def normed(xr, mr, rows):
        return _rmsnorm_mod(xr[rows, :], nw_ref[0], mr[0, :, d:2 * d], mr[0, :, 0:d]).astype(BF16)

    @pl.when(step == 0)
    def _():
        h_ref[0] = normed(x_ref, mod_ref, slice(None))

    half = INPROJ_ROWS // 2
    for j in range(N_PCHUNKS):
        if j < N_HEAD_CHUNKS:
            p_ref[j] = (_dot_nt(h_ref[cur], w_ref[0, j * PCHUNK:(j + 1) * PCHUNK, :]) * qs_ref[j]).astype(BF16)
        else:
            t = j - N_HEAD_CHUNKS
            dst = CH_HG_Q + t if t < N_TAIL_CHUNKS - N_GATE_CHUNKS else CH_GATES + t - (N_TAIL_CHUNKS - N_GATE_CHUNKS)
            p_ref[dst] = _dot_nt(h_ref[cur], w_ref[0, tail0 + t * PCHUNK:tail0 + (t + 1) * PCHUNK, :]).astype(BF16)
        if j in INPROJ_NORM_AFTER:
            rows = slice(0, half) if j == INPROJ_NORM_AFTER[0] else slice(half, INPROJ_ROWS)
            h_ref[1 - cur, rows, :] = normed(xn_ref, modn_ref, rows)
    lr = _dot_nt(h_ref[cur], w_ref[0, lr0:tail0, :])
    lr_ref[...] = jnp.concatenate([lr, jnp.zeros((INPROJ_ROWS, LR_PAD - 2 * GLA_RANK), F32)], axis=1).astype(BF16)


def _in_projection(x2, mod_l, norm_w, q_scale, w_t, layer, tokens_per_batch):
    n, d = x2.shape
    tm = INPROJ_ROWS
    per_batch = tokens_per_batch // tm
    last = n // tm - 1
    nxt = lambda i: jnp.minimum(i + 1, last)
    return pl.pallas_call(
        _inproj_kernel,
        grid=(n // tm,),
        in_specs=[
            pl.BlockSpec((tm, d), lambda i: (i, 0)),
            pl.BlockSpec((tm, d), lambda i: (nxt(i), 0)),
            pl.BlockSpec((1, 1, N_MOD * d), lambda i: (i // per_batch, 0, 0)),
            pl.BlockSpec((1, 1, N_MOD * d), lambda i: (nxt(i) // per_batch, 0, 0)),
            _layer_resident(norm_w.shape, layer),
            _resident(q_scale.shape),
            _layer_resident(w_t.shape, layer),
        ],
        out_specs=[
            pl.BlockSpec((N_PCHUNKS, tm, PCHUNK), lambda i: (0, i, 0)),
            pl.BlockSpec((tm, LR_PAD), lambda i: (i, 0)),
        ],
        out_shape=[
            jax.ShapeDtypeStruct((N_PCHUNKS, n, PCHUNK), BF16),
            jax.ShapeDtypeStruct((n, LR_PAD), BF16),
        ],
        scratch_shapes=[pltpu.VMEM((2, tm, d), BF16)],
        compiler_params=_cparams(("arbitrary",)),
        name="in_proj",
    )(x2, x2, mod_l, mod_l, norm_w, q_scale, w_t)


NA_ROWS_PER_STEP = 8
NA_ROW_UNROLL = 4
NA_BAND = WIN_R * GRID_W
NA_VARIANTS = 8


def _na_kernel(q_ref, k_ref, v_ref, bias_ref, o_ref, *, rows_total):
    step = pl.program_id(1)
    lane = lax.broadcasted_iota(jnp.int32, (GRID_W, HEAD_LANES), 1)
    low = lane < NA_HEAD_DIM

    def row_group(gi, carry):
        items = []
        for u in range(NA_ROW_UNROLL):
            rr = gi * NA_ROW_UNROLL + u
            r = step * NA_ROWS_PER_STEP + rr
            row_start = jnp.clip(r - WIN_R // 2, 0, rows_total - WIN_R)
            variant = jnp.where(r < WIN_R // 2, r,
                                jnp.where(r > rows_total - WIN_R // 2, r - (rows_total - WIN_R), WIN_R // 2))
            k0 = pl.multiple_of(row_start * GRID_W, GRID_W)
            q0 = pl.multiple_of(rr * GRID_W, GRID_W)
            for hp in range(NA_HEADS // 2):
                items.append((q0, k0, variant, hp, slice(hp * HEAD_LANES, (hp + 1) * HEAD_LANES)))
        scores = []
        for q0, k0, variant, hp, hs in items:
            qp = q_ref[0, pl.ds(q0, GRID_W), hs]
            zero = jnp.zeros_like(qp)
            q2 = jnp.concatenate([jnp.where(low, qp, zero), jnp.where(low, zero, qp)], axis=0)
            scores.append(_dot_nt(q2, k_ref[0, pl.ds(k0, NA_BAND), hs]) + bias_ref[0, variant, hp])
        probs = []
        for s in scores:
            e = jnp.exp(s - jnp.max(s, axis=-1, keepdims=True))
            probs.append((e.astype(BF16), jnp.sum(e, axis=-1, keepdims=True)))
        for (q0, k0, variant, hp, hs), (e16, denom) in zip(items, probs):
            o2 = _dot(e16, v_ref[0, pl.ds(k0, NA_BAND), hs]) / denom
            o_ref[pl.ds(q0, GRID_W), hs] = jnp.where(low, o2[:GRID_W], o2[GRID_W:]).astype(BF16)
        return carry

    lax.fori_loop(0, NA_ROWS_PER_STEP // NA_ROW_UNROLL, row_group, 0)


def _na_bias_tables(rpb, rows):
    depth = rpb.shape[0]
    wr = min(WIN_R, rows)
    col = np.arange(GRID_W)
    col_start = np.clip(col - WIN_C // 2, 0, GRID_W - WIN_C)
    col_mask = (col[None, :] >= col_start[:, None]) & (col[None, :] < col_start[:, None] + WIN_C)
    col_off = np.clip(col[None, :] - col[:, None], -(WIN_C - 1), WIN_C - 1) + (WIN_C - 1)
    n_ro, n_co = 2 * WIN_R - 1, 2 * WIN_C - 1
    onehot = (col_off.reshape(-1)[None, :] == np.arange(n_co)[:, None]).astype(np.float32)
    by_col = jnp.dot(rpb.astype(F32).reshape(depth * NA_HEADS * n_ro, n_co), onehot, precision=lax.Precision.HIGHEST)
    by_col = by_col.reshape(depth, NA_HEADS, n_ro, GRID_W, GRID_W)
    by_col = jnp.where(col_mask[None, None, None], by_col, NEG_INF)
    rep_rows = list(range(WIN_R // 2)) + [WIN_R // 2] + [rows - WIN_R + v for v in range(WIN_R // 2 + 1, WIN_R)]
    tabs = []
    for r in rep_rows:
        row_start = int(np.clip(r - WIN_R // 2, 0, rows - wr))
        ro0 = row_start - r + (WIN_R - 1)
        tabs.append(by_col[:, :, ro0:ro0 + wr])
    tabs = jnp.stack(tabs, axis=1).transpose(0, 1, 2, 4, 3, 5)
    return tabs.reshape(depth, NA_VARIANTS, NA_HEADS // 2, 2 * GRID_W, wr * GRID_W)


def _neighborhood_attention(p3, bias_tabs, layer, bsz, t):
    rows = t // GRID_W
    assert rows >= 2 * WIN_R and rows % NA_ROWS_PER_STEP == 0
    n = bsz * t
    tq = NA_ROWS_PER_STEP * GRID_W
    steps = t // tq
    return pl.pallas_call(
        functools.partial(_na_kernel, rows_total=rows),
        grid=(bsz, steps),
        in_specs=[
            pl.BlockSpec((1, tq, PCHUNK), lambda b, i: (CH_NA_Q, b * steps + i, 0)),
            pl.BlockSpec((1, t, PCHUNK), lambda b, i: (CH_NA_K, b, 0)),
            pl.BlockSpec((1, t, PCHUNK), lambda b, i: (CH_NA_V, b, 0)),
            _layer_resident(bias_tabs.shape, layer),
        ],
        out_specs=pl.BlockSpec((tq, NA_WIDTH), lambda b, i: (b * steps + i, 0)),
        out_shape=jax.ShapeDtypeStruct((n, NA_WIDTH), BF16),
        compiler_params=_cparams(("arbitrary", "arbitrary")),
        name="neighborhood_attn",
    )(p3, p3, p3, bias_tabs)


def _rec_kernel(*refs, mode, reverse, final, chunk, block, unroll):
    it = iter(refs)
    q_ref = next(it)
    if mode == "gla":
        v_ref, lr_ref, uph_ref, upl_ref, gbias_ref = (next(it) for _ in range(5))
    else:
        f_ref, v_ref, lb_ref = (next(it) for _ in range(3))
    tri_ref = next(it)
    if final:
        prev_ref, gate_ref, nw_ref = (next(it) for _ in range(3))
    o_ref = next(it)
    st_ref, st_old, kbuf, wbuf, qbuf = (next(it) for _ in range(5))

    c, s = chunk, block
    nb = c // s
    dk = GLA_DK if mode == "gla" else HEAD_LANES
    width = N_REC_HEADS * dk
    n_chunks = REC_TOKENS_PER_STEP // c

    def key_lanes(h):
        if dk == HEAD_LANES:
            return slice(h * HEAD_LANES, (h + 1) * HEAD_LANES), None
        t = (h * dk) // HEAD_LANES
        lane = lax.broadcasted_iota(jnp.int32, (1, HEAD_LANES), 1)
        mine = (lane // dk) == (h - t * (HEAD_LANES // dk))
        return slice(t * HEAD_LANES, (t + 1) * HEAD_LANES), mine

    def own(x, mine):
        return x if mine is None else jnp.where(mine, x, jnp.zeros_like(x))

    @pl.when(pl.program_id(1) == 0)
    def _():
        st_ref[...] = jnp.zeros_like(st_ref)

    ii = lax.broadcasted_iota(jnp.int32, (c, c), 0)
    jj = lax.broadcasted_iota(jnp.int32, (c, c), 1)
    bi, bj = ii // s, jj // s
    gap = (bj - bi) if reverse else (bi - bj)
    seen = (jj >= ii) if reverse else (jj <= ii)
    cls = jnp.where(gap == 0, jnp.where(seen, 0, -1), jnp.where(gap > 0, gap, -1))
    row_id = lax.broadcasted_iota(jnp.int32, (c, 1), 0)
    col_id = lax.broadcasted_iota(jnp.int32, (1, c), 1)

    def prev_block(i, steps=1):
        return i + steps if reverse else i - steps

    def per_block(vals):
        return jnp.concatenate([jnp.broadcast_to(x, (s, width)) for x in vals], axis=0)

    def chunk_inputs(rows):
        if mode == "gla":
            q = q_ref[0, rows, :width].astype(F32)
            k = q_ref[0, rows, width:].astype(F32)
            lr = lr_ref[rows, :]
            z = _dot(lr, uph_ref[...]) + _dot(lr, upl_ref[...]) + gbias_ref[...]
            g = (jnp.minimum(z, 0.0) - jnp.log(1.0 + jnp.exp(-jnp.abs(z)))) * (1.0 / GLA_TAU)
        else:
            z = f_ref[0, rows, :].astype(F32)
            sig, sig_neg = _sigmoid_pair(z)
            lb = lb_ref[...]
            g = jnp.log(jnp.maximum(lb + (1.0 - lb) * sig, F_FLOOR))
            k = (1.0 - lb) * sig_neg
            q = _silu(q_ref[0, rows, :].astype(F32))
        return q, k, g

    def chunk_factors(q, k, w, exact):
        w3 = w.reshape(nb, s, width)
        last = 0 if reverse else s - 1
        wb = [w3[i, last:last + 1, :] for i in range(nb)]
        order = list(range(nb - 1, -1, -1)) if reverse else list(range(nb))
        beta = [None] * nb
        run = jnp.zeros((1, width), F32)
        for i in order:
            beta[i] = run
            run = run + wb[i]
        total = run
        ewb = [jnp.exp(x) for x in wb]
        qd = q * jnp.exp(w)
        ks = k * jnp.exp(per_block(wb) - w)
        kd16 = None if exact else (ks * per_block([jnp.exp(-x) for x in wb])).astype(BF16)
        one = jnp.ones((1, width), F32)
        lhs, fac = [qd.astype(BF16)], [one] * nb
        for dgap in range(2, nb):
            fac = [fac[i] * (ewb[prev_block(i, dgap - 1)] if 0 <= prev_block(i, dgap - 1) < nb else one)
                   for i in range(nb)]
            lhs.append((qd * per_block(fac)).astype(BF16))
        ks16 = ks.astype(BF16)
        if nb == 1:
            return lhs, kd16, ks16, lhs[0], ks16, jnp.exp(total)
        qe16 = (qd * jnp.exp(per_block(beta))).astype(BF16)
        ke16 = (ks * jnp.exp(per_block([total - beta[i] - wb[i] for i in range(nb)]))).astype(BF16)
        return lhs, kd16, ks16, qe16, ke16, jnp.exp(total)

    def head_scores(h, lhs, kd16, ks16, sd):
        ks_, mine = key_lanes(h)
        if sd is None:
            sd = _dot_nt(own(lhs[0][:, ks_], mine), kd16[:, ks_])
        a = jnp.where(cls == 0, sd, 0.0)
        if nb > 1:
            def far_rows(dgap):
                return slice(0, c - dgap * s) if reverse else slice(dgap * s, c)

            parts = [own(lhs[max(dgap - 1, 0)][:, ks_], mine)[far_rows(dgap)] for dgap in range(1, nb)]
            so = _dot_nt(jnp.concatenate(parts, axis=0), ks16[:, ks_])
            start = 0
            for dgap in range(1, nb):
                n_far = c - dgap * s
                blank = jnp.zeros((dgap * s, c), F32)
                part = so[start:start + n_far]
                start += n_far
                full = jnp.concatenate([part, blank] if reverse else [blank, part], axis=0)
                a = jnp.where(cls == dgap, full, a)
        return a.astype(BF16)

    def chunk_output(h, a16, qe16, v16, st):
        ks_, mine = key_lanes(h)
        lhs = jnp.concatenate([own(qe16[:, ks_], mine), a16], axis=1)
        return _dot(lhs, jnp.concatenate([st.astype(BF16), v16], axis=0))

    def emit(h, rows, o):
        hs = slice(h * HEAD_LANES, (h + 1) * HEAD_LANES)
        if final:
            tot = prev_ref[rows, hs] + o
            ms = jnp.mean(tot * tot, axis=-1, keepdims=True)
            y = tot * lax.rsqrt(ms + RMS_EPS) * nw_ref[...]
            o_ref[rows, hs] = (y * _silu(gate_ref[0, rows, hs].astype(F32))).astype(o_ref.dtype)
        else:
            o_ref[rows, hs] = o

    def exact_same_block_scores(u, h):
        hs, mine = key_lanes(h)

        def col_body(jg, acc):
            j0 = pl.multiple_of(jg * SUBLANES, SUBLANES)
            k8 = kbuf[u, pl.ds(j0, SUBLANES), hs]
            w8 = wbuf[u, pl.ds(j0, SUBLANES), hs]
            for jr in range(SUBLANES):
                j = j0 + jr
                same = (row_id // s) == (j // s)
                valid = same & ((row_id <= j) if reverse else (row_id >= j))
                e = jnp.exp(jnp.where(valid, wbuf[u, :, hs] - w8[jr:jr + 1], 0.0))
                t = jnp.sum(own(qbuf[u, :, hs], mine) * k8[jr:jr + 1] * e, axis=-1, keepdims=True)
                acc = jnp.where(valid & (col_id == j), t, acc)
            return acc

        return lax.fori_loop(0, c // SUBLANES, col_body, jnp.zeros((c, c), F32))

    def group_body(gi, carry):
        all_rows = []
        for u in range(unroll):
            ci = gi * unroll + u
            cc = (n_chunks - 1 - ci) if reverse else ci
            all_rows.append(pl.ds(pl.multiple_of(cc * c, c), c))
        ws, facs = [], []
        for u, rows in enumerate(all_rows):
            q, k, g = chunk_inputs(rows)
            g_hi, g_lo = _split_bf16(g)
            w2 = _dot(tri_ref[...], jnp.concatenate([g_hi, g_lo], axis=1))
            w = w2[:, :width] + w2[:, width:]
            ws.append(w)
            facs.append(chunk_factors(q, k, w, exact=False))
            qbuf[u] = q
            kbuf[u] = k
            wbuf[u] = w
        a16 = [[head_scores(h, f[0], f[1], f[2], None) for h in range(N_REC_HEADS)] for f in facs]
        vs, incs = [], []
        for u, rows in enumerate(all_rows):
            ke16 = facs[u][4]
            vs.append([v_ref[0, rows, h * HEAD_LANES:(h + 1) * HEAD_LANES] for h in range(N_REC_HEADS)])
            incs.append([_dot_tn(own(ke16[:, key_lanes(h)[0]], key_lanes(h)[1]), vs[u][h])
                         for h in range(N_REC_HEADS)])
        states = []
        for h in range(N_REC_HEADS):
            ks_, _ = key_lanes(h)
            st = st_ref[h]
            states.append([])
            for u in range(unroll):
                states[h].append(st)
                st_old[u, h] = st
                decay_col = jnp.broadcast_to(facs[u][5][:, ks_], (HEAD_LANES, HEAD_LANES)).T
                st = st * decay_col + incs[u][h]
            st_ref[h] = st
        for u, rows in enumerate(all_rows):
            for h in range(N_REC_HEADS):
                emit(h, rows, chunk_output(h, a16[u][h], facs[u][3], vs[u][h], states[h][u]))

        w_min = functools.reduce(jnp.minimum, ws)

        @pl.when(jnp.min(w_min) < -DECAY_SPAN_LIMIT)
        def _():
            for u, rows in enumerate(all_rows):
                lhs, _, ks16, qe16, _, _ = chunk_factors(qbuf[u], kbuf[u], wbuf[u], exact=True)
                for h in range(N_REC_HEADS):
                    hs = slice(h * HEAD_LANES, (h + 1) * HEAD_LANES)
                    a = head_scores(h, lhs, None, ks16, exact_same_block_scores(u, h))
                    emit(h, rows, chunk_output(h, a, qe16, v_ref[0, rows, hs], st_old[u, h]))

        return carry

    lax.fori_loop(0, n_chunks // unroll, group_body, 0)


def _scan_mask(chunk, block, reverse):
    i = np.arange(chunk)
    same = (i[:, None] // block) == (i[None, :] // block)
    seen = (i[None, :] >= i[:, None]) if reverse else (i[None, :] <= i[:, None])
    return jnp.asarray((same & seen).astype(np.float32), BF16)


def _recurrent_pass(mode, reverse, final, chunk, block, p3, bsz, t, extra, prev=None, norm_w=None):
    n = bsz * t
    tb = REC_TOKENS_PER_STEP
    nblk = t // tb
    width = N_REC_HEADS * HEAD_LANES
    key_width = N_REC_HEADS * (GLA_DK if mode == "gla" else HEAD_LANES)
    unroll = min(tb // chunk, REC_MAX_UNROLL)

    def tok(b, i):
        return b * nblk + ((nblk - 1 - i) if reverse else i)

    def chunk_spec(ch):
        return pl.BlockSpec((1, tb, PCHUNK), lambda b, i: (ch, tok(b, i), 0))

    tok_spec = lambda w: pl.BlockSpec((tb, w), lambda b, i: (tok(b, i), 0))
    operands, specs = [], []
    if mode == "gla":
        lr, up_hi, up_lo, gbias = extra
        operands += [p3, p3, lr, up_hi, up_lo, gbias]
        specs += [chunk_spec(CH_GLA_QK), chunk_spec(CH_GLA_V), tok_spec(LR_PAD),
                  _resident(up_hi.shape), _resident(up_lo.shape), _resident(gbias.shape)]
        gate_ch = CH_GLA_G
    else:
        (lb,) = extra
        operands += [p3, p3, p3, lb]
        specs += [chunk_spec(CH_HG_Q), chunk_spec(CH_HG_FB if reverse else CH_HG_FF), chunk_spec(CH_HG_I),
                  _resident(lb.shape)]
        gate_ch = CH_HG_G
    tri = _scan_mask(chunk, block, reverse)
    operands.append(tri)
    specs.append(_resident(tri.shape))
    if final:
        operands += [prev, p3, norm_w]
        specs += [tok_spec(width), chunk_spec(gate_ch), _resident(norm_w.shape)]
    out_dtype = BF16 if final else F32
    return pl.pallas_call(
        functools.partial(_rec_kernel, mode=mode, reverse=reverse, final=final, chunk=chunk, block=block,
                          unroll=unroll),
        grid=(bsz, nblk),
        in_specs=specs,
        out_specs=tok_spec(width),
        out_shape=jax.ShapeDtypeStruct((n, width), out_dtype),
        scratch_shapes=[
            pltpu.VMEM((N_REC_HEADS, HEAD_LANES, HEAD_LANES), F32),
            pltpu.VMEM((unroll,N_REC_HEADS, HEAD_LANES, HEAD_LANES), F32),
            pltpu.VMEM((unroll,chunk, key_width), F32),
            pltpu.VMEM((unroll,chunk, key_width), F32),
            pltpu.VMEM((unroll,chunk, key_width), F32),
        ],
        compiler_params=_cparams(("arbitrary", "arbitrary")),
        name=f"{mode}_{'bwd' if reverse else 'fwd'}",
    )(*operands)


GLA_CHUNK, GLA_BLOCK = 128, 128
HGRN_CHUNK, HGRN_BLOCK = 64, 32


def _gla_branch(p3, lr, up_pads, gbias_pads, norm_w, bsz, t):
    fwd = _recurrent_pass("gla", False, False, GLA_CHUNK, GLA_BLOCK, p3, bsz, t,
                          (lr, up_pads[0][0], up_pads[0][1], gbias_pads[0]))
    return _recurrent_pass("gla", True, True, GLA_CHUNK, GLA_BLOCK, p3, bsz, t,
                           (lr, up_pads[1][0], up_pads[1][1], gbias_pads[1]), prev=fwd, norm_w=norm_w)


def _hgrn_branch(p3, lbs, norm_w, bsz, t):
    fwd = _recurrent_pass("hgrn", False, False, HGRN_CHUNK, HGRN_BLOCK, p3, bsz, t, (lbs[0],))
    return _recurrent_pass("hgrn", True, True, HGRN_CHUNK, HGRN_BLOCK, p3, bsz, t, (lbs[1],),
                           prev=fwd, norm_w=norm_w)


MLP_ROWS = 512
MLP_GROUP_ROWS = 256


def _mlp_kernel(x_ref, ona_ref, ogla_ref, ohg_ref, gates_ref, mod_ref, wna_ref, wgla_ref, whg_ref, wout_ref,
                n2_ref, wg_ref, wu_ref, wd_ref, fn_ref, o_ref, *, last):
    d = D_MODEL
    gate1 = mod_ref[0, :, 2 * d:3 * d]
    shift2 = mod_ref[0, :, 3 * d:4 * d]
    scale2 = mod_ref[0, :, 4 * d:5 * d]
    gate2 = mod_ref[0, :, 5 * d:6 * d]
    groups = [slice(i * MLP_GROUP_ROWS, (i + 1) * MLP_GROUP_ROWS) for i in range(MLP_ROWS // MLP_GROUP_ROWS)]

    def gate(rs, i):
        g = jnp.concatenate([gates_ref[2 * i, rs, :].astype(F32), gates_ref[2 * i + 1, rs, :].astype(F32)], axis=1)
        return _sigmoid(g)

    merged = [(gate(rs, 0) * _dot(ona_ref[rs, :], wna_ref[0])
               + gate(rs, 1) * _dot(ogla_ref[rs, :], wgla_ref[0])
               + gate(rs, 2) * _dot(ohg_ref[rs, :], whg_ref[0])).astype(BF16) for rs in groups]
    x1 = [x_ref[rs, :] + gate1 * _dot(m, wout_ref[0]) for rs, m in zip(groups, merged)]
    h = [_rmsnorm_mod(x, n2_ref[0], scale2, shift2).astype(BF16) for x in x1]
    act = [(_silu(_dot(hh, wg_ref[0])) * _dot(hh, wu_ref[0])).astype(BF16) for hh in h]
    for rs, x, a in zip(groups, x1, act):
        x2 = x + gate2 * _dot(a, wd_ref[0])
        if last:
            ms = jnp.mean(x2 * x2, axis=-1, keepdims=True)
            x2 = x2 * lax.rsqrt(ms + RMS_EPS) * fn_ref[...]
        o_ref[rs, :] = x2


def _merge_mlp(x2, o_na, o_gla, o_hg, p3, mod_l, layer_wts, final_norm_w, layer, tokens_per_batch, last):
    n, d = x2.shape
    tm = MLP_ROWS
    per_batch = tokens_per_batch // tm
    row = lambda w: pl.BlockSpec((tm, w), lambda i: (i, 0))
    return pl.pallas_call(
        functools.partial(_mlp_kernel, last=last),
        grid=(n // tm,),
        in_specs=[
            row(d), row(NA_WIDTH), row(NA_WIDTH), row(NA_WIDTH),
            pl.BlockSpec((N_GATE_CHUNKS, tm, PCHUNK), lambda i: (CH_GATES // N_GATE_CHUNKS, i, 0)),
            pl.BlockSpec((1, 1, N_MOD * d), lambda i: (i // per_batch, 0, 0)),
        ] + [_layer_resident(w.shape, layer) for w in layer_wts] + [_resident(final_norm_w.shape)],
        out_specs=row(d),
        out_shape=jax.ShapeDtypeStruct((n, d), F32),
        compiler_params=_cparams(("arbitrary",)),
        name="merge_mlp",
    )(x2, o_na, o_gla, o_hg, p3, mod_l, *layer_wts, final_norm_w)


def _transposed_w_in(w):
    assert w.shape[2] == N_PCHUNKS * PCHUNK + 2 * GLA_RANK
    return jnp.swapaxes(w, 1, 2).astype(BF16)


def _query_scales():
    qs = np.ones((N_HEAD_CHUNKS, 1, PCHUNK), np.float32)
    qs[CH_NA_Q] = NA_HEAD_DIM ** -0.5
    qs[CH_GLA_QK, :, :GLA_HEADS * GLA_DK] = GLA_DK ** -0.5
    return jnp.asarray(qs)


def kernel(x, c, w_ada, b_ada, norm1_w, w_in, na_rpb, gla_lr_up, gla_lr_bias, gla_norm_w, hgrn_lb_logits, hgrn_norm_w, w_proj_na, w_proj_gla, w_proj_hgrn, w_out, norm2_w, w_ffn_gate, w_ffn_up, w_ffn_down, final_norm_w):
    bsz, t, d = x.shape
    depth = w_in.shape[0]
    assert d == D_MODEL and t % REC_TOKENS_PER_STEP == 0 and t % INPROJ_ROWS == 0
    n = bsz * t

    mods = _modulation(c, w_ada, b_ada).reshape(depth, bsz, 1, N_MOD * d)
    lb_p = jax.nn.softmax(hgrn_lb_logits.astype(F32), axis=0)
    lb_all = jnp.clip(jnp.cumsum(lb_p, axis=0) - lb_p[0], 0.0, 1.0)

    w_t = _transposed_w_in(w_in)
    q_scale = _query_scales()
    bias_tabs = _na_bias_tables(na_rpb, t // GRID_W)
    norm1 = norm1_w.reshape(depth, 1, d)
    mlp_wts = [w_proj_na.astype(BF16), w_proj_gla.astype(BF16), w_proj_hgrn.astype(BF16), w_out.astype(BF16),
               norm2_w.reshape(depth, 1, d), w_ffn_gate.astype(BF16), w_ffn_up.astype(BF16),
               w_ffn_down.astype(BF16)]
    up = jnp.zeros((depth, 2, LR_PAD, GLA_HEADS * GLA_DK), F32)
    for s in range(2):
        up = up.at[:, s, s * GLA_RANK:(s + 1) * GLA_RANK].set(gla_lr_up[:, s])
    up_hi = up.astype(BF16)
    up_lo = (up - up_hi.astype(F32)).astype(BF16)

    xf = x.reshape(n, d)
    for l in range(depth):
        p3, lr = _in_projection(xf, mods[l], norm1, q_scale, w_t, l, t)

        o_na = _neighborhood_attention(p3, bias_tabs, l, bsz, t)

        up_pads = [(up_hi[l, s], up_lo[l, s]) for s in range(2)]
        gbias = [gla_lr_bias[l, s].reshape(1, -1) for s in range(2)]
        o_gla = _gla_branch(p3, lr, up_pads, gbias, gla_norm_w[l].reshape(1, -1), bsz, t)

        lbs = [lb_all[l, s * HGRN_KEY_WIDTH:(s + 1) * HGRN_KEY_WIDTH].reshape(1, -1) for s in range(2)]
        o_hg = _hgrn_branch(p3, lbs, hgrn_norm_w[l].reshape(1, -1), bsz, t)

        xf = _merge_mlp(xf, o_na, o_gla, o_hg, p3, mods[l], mlp_wts, final_norm_w.reshape(1, d), l, t,
                        last=(l == depth - 1))
    return xf.reshape(bsz, t, d)
```
